```python
import math
import jax
import jax.numpy as jnp
from jax import lax
import numpy as np

D_MODEL = 1024
BATCH = 8
SEQ = 4096
DEPTH = 1

D_INNER = 2 * D_MODEL
D_ATTN = D_INNER // 2
D_SSM = D_INNER - D_ATTN
SB_HEAD_DIM = 64
SB_HEADS = D_ATTN // SB_HEAD_DIM
SSM_HEAD_DIM = 64
SSM_HEADS = D_SSM // SSM_HEAD_DIM
SSM_GROUPS = 2
SSM_STATE = 128
CONV_WIDTH = 4
SSD_CHUNK = 128
Q_BLOCK = 128
NORM_EPS = 1e-6
D_XBC = D_SSM + 2 * SSM_GROUPS * SSM_STATE
D_PROJ = 4 * D_ATTN + D_XBC + SSM_HEADS + D_SSM
DT_MIN = 1e-3
DT_MAX = 1e-1

kernel_name = "hybrid_stickbreak_ssd_layer"


def rms_norm(x, gain):
    xf = x.astype(jnp.float32)
    y = xf * lax.rsqrt(jnp.mean(xf * xf, axis=-1, keepdims=True) + NORM_EPS)
    return (y * gain.astype(jnp.float32)).astype(x.dtype)


def stick_breaking_attention(q, k, v):
    seq = q.shape[1]
    scale = q.shape[-1] ** -0.5
    outs = []
    for blk in range(seq // Q_BLOCK):
        start = blk * Q_BLOCK
        end = start + Q_BLOCK
        z = jnp.einsum("bqhd,bkhd->bhqk", q[:, start:end], k[:, :end]).astype(jnp.float32) * scale
        t_idx = start + jnp.arange(Q_BLOCK)[:, None]
        s_idx = jnp.arange(end)[None, :]
        mask = s_idx < t_idx
        log_beta = jax.nn.log_sigmoid(z)
        log_one_minus = jnp.where(mask, jax.nn.log_sigmoid(-z), 0.0)
        tail = lax.cumsum(log_one_minus, axis=3, reverse=True) - log_one_minus
        weights = jnp.exp(jnp.where(mask, log_beta + tail, -jnp.inf))
        outs.append(jnp.einsum("bhqk,bkhd->bqhd", weights.astype(v.dtype), v[:, :end]))
    return jnp.concatenate(outs, axis=1)


def causal_depthwise_conv(u, w, b):
    ch = u.shape[-1]
    y = lax.conv_general_dilated(
        u, w[:, None, :].astype(u.dtype), window_strides=(1,),
        padding=[(CONV_WIDTH - 1, 0)], dimension_numbers=("NWC", "WIO", "NWC"),
        feature_group_count=ch)
    return y + b.astype(u.dtype)


def ssd_scan(xs, dt, a, b_in, c_in, d_skip):
    bsz, seq, nh, hp = xs.shape
    ng, ns = b_in.shape[-2:]
    hpg = nh // ng
    nc = seq // SSD_CHUNK
    cl = SSD_CHUNK
    xf = xs.astype(jnp.float32)
    xdt = (xf * dt[..., None]).reshape(bsz, nc, cl, ng, hpg, hp)
    bc = b_in.astype(jnp.float32).reshape(bsz, nc, cl, ng, ns)
    cc = c_in.astype(jnp.float32).reshape(bsz, nc, cl, ng, ns)
    log_decay = (dt * a).reshape(bsz, nc, cl, ng, hpg).transpose(0, 1, 3, 4, 2)
    a_cum = jnp.cumsum(log_decay, axis=-1)
    causal = jnp.tril(jnp.ones((cl, cl), dtype=bool))
    seg = a_cum[..., :, None] - a_cum[..., None, :]
    decay = jnp.exp(jnp.where(causal, seg, -jnp.inf))
    cb = jnp.einsum("bclgn,bcsgn->bcgls", cc, bc)
    y_diag = jnp.einsum("bcgrls,bcsgrp->bclgrp", cb[:, :, :, None] * decay, xdt)
    decay_to_end = jnp.exp(a_cum[..., -1:] - a_cum)
    chunk_states = jnp.einsum("bclgn,bcgrl,bclgrp->bcgrpn", bc, decay_to_end, xdt)
    chunk_decay = jnp.exp(a_cum[..., -1])

    def step(h, inp):
        st, dec = inp
        return h * dec[..., None, None] + st, h

    h0 = jnp.zeros((bsz, ng, hpg, hp, ns), jnp.float32)
    _, prev = lax.scan(step, h0, (jnp.moveaxis(chunk_states, 1, 0), jnp.moveaxis(chunk_decay, 1, 0)))
    prev = jnp.moveaxis(prev, 0, 1)
    y_off = jnp.einsum("bclgn,bcgrpn,bcgrl->bclgrp", cc, prev, jnp.exp(a_cum))
    y = (y_diag + y_off).reshape(bsz, seq, nh, hp)
    return y + xf * d_skip.astype(jnp.float32)[:, None]


def setup_inputs(seed: int = 0) -> dict:
    key = jax.random.key(seed)
    ks = jax.random.split(key, 16)
    f = jnp.float32
    nrm = jax.random.normal
    x = nrm(ks[0], (BATCH, SEQ, D_MODEL), f)
    c = nrm(ks[1], (BATCH, D_MODEL), f)
    w_ada = nrm(ks[2], (DEPTH, D_MODEL, 3 * D_MODEL), f) * D_MODEL ** -0.5
    b_ada = 0.02 * nrm(ks[3], (DEPTH, 3 * D_MODEL), f)
    norm_in_gain = 1.0 + 0.1 * nrm(ks[4], (DEPTH, D_MODEL), f)
    w_in = nrm(ks[5], (DEPTH, D_MODEL, D_PROJ), f) * D_MODEL ** -0.5
    conv_w = nrm(ks[6], (DEPTH, CONV_WIDTH, D_XBC), f) * CONV_WIDTH ** -0.5
    conv_b = 0.02 * nrm(ks[7], (DEPTH, D_XBC), f)
    dt0 = jnp.exp(jax.random.uniform(ks[8], (DEPTH, SSM_HEADS), f, math.log(DT_MIN), math.log(DT_MAX)))
    dt_bias = dt0 + jnp.log(-jnp.expm1(-dt0))
    a_log = jnp.log(jax.random.uniform(ks[9], (DEPTH, SSM_HEADS), f, 1.0, 16.0))
    d_skip = 1.0 + 0.1 * nrm(ks[10], (DEPTH, SSM_HEADS), f)
    sb_norm_gain = 1.0 + 0.1 * nrm(ks[11], (DEPTH, D_ATTN), f)
    ssm_norm_gain = 1.0 + 0.1 * nrm(ks[12], (DEPTH, D_SSM), f)
    w_out = nrm(ks[13], (DEPTH, D_INNER, D_MODEL), f) * D_INNER ** -0.5
    norm_f_gain = 1.0 + 0.1 * nrm(ks[14], (D_MODEL,), f)
    return {"x": x, "c": c, "w_ada": w_ada, "b_ada": b_ada, "norm_in_gain": norm_in_gain,
            "w_in": w_in, "conv_w": conv_w, "conv_b": conv_b, "dt_bias": dt_bias,
            "a_log": a_log, "d_skip": d_skip, "sb_norm_gain": sb_norm_gain,
            "ssm_norm_gain": ssm_norm_gain, "w_out": w_out, "norm_f_gain": norm_f_gain}


def reference(x, c, w_ada, b_ada, norm_in_gain, w_in, conv_w, conv_b, dt_bias, a_log,
              d_skip, sb_norm_gain, ssm_norm_gain, w_out, norm_f_gain):
    bsz, seq, _ = x.shape
    splits = [D_ATTN, 2 * D_ATTN, 3 * D_ATTN, 4 * D_ATTN, 4 * D_ATTN + D_XBC,
              4 * D_ATTN + D_XBC + SSM_HEADS]
    c_act = jax.nn.silu(c)
    for layer in range(DEPTH):
        mod = c_act @ w_ada[layer] + b_ada[layer]
        shift, scale, gate = jnp.split(mod, 3, axis=-1)
        h = rms_norm(x, norm_in_gain[layer]) * (1.0 + scale[:, None, :]) + shift[:, None, :]

        proj = h @ w_in[layer]
        q, k, v, z_attn, xbc, dt_raw, z_ssm = jnp.split(proj, splits, axis=-1)

        o = stick_breaking_attention(
            q.reshape(bsz, seq, SB_HEADS, SB_HEAD_DIM),
            k.reshape(bsz, seq, SB_HEADS, SB_HEAD_DIM),
            v.reshape(bsz, seq, SB_HEADS, SB_HEAD_DIM)).reshape(bsz, seq, D_ATTN)
        y_attn = rms_norm(o, sb_norm_gain[layer]) * jax.nn.silu(z_attn)

        xbc = jax.nn.silu(causal_depthwise_conv(xbc, conv_w[layer], conv_b[layer]))
        xs, b_ssm, c_ssm = jnp.split(xbc, [D_SSM, D_SSM + SSM_GROUPS * SSM_STATE], axis=-1)
        dt = jax.nn.softplus((dt_raw + dt_bias[layer]).astype(jnp.float32))
        a = -jnp.exp(a_log[layer].astype(jnp.float32))
        y = ssd_scan(xs.reshape(bsz, seq, SSM_HEADS, SSM_HEAD_DIM), dt, a,
                     b_ssm.reshape(bsz, seq, SSM_GROUPS, SSM_STATE),
                     c_ssm.reshape(bsz, seq, SSM_GROUPS, SSM_STATE), d_skip[layer])
        y = y.reshape(bsz, seq, D_SSM).astype(x.dtype)
        y_ssm = rms_norm(y * jax.nn.silu(z_ssm), ssm_norm_gain[layer])

        mixed = jnp.concatenate([y_attn, y_ssm], axis=-1) @ w_out[layer]
        x = x + gate[:, None, :] * mixed
    return rms_norm(x, norm_f_gain)
```

```python
import functools

import jax
import jax.numpy as jnp
from jax import lax
from jax.experimental import pallas as pl
from jax.experimental.pallas import tpu as pltpu

F32 = jnp.float32
BF16 = jnp.bfloat16

D_MODEL = 1024
D_ATTN = 1024
D_SSM = 1024
HEAD_DIM = 64
SSM_HEADS = 16
SSM_GROUPS = 2
SSM_STATE = 128
CONV_WIDTH = 4
D_XBC = D_SSM + 2 * SSM_GROUPS * SSM_STATE
NORM_EPS = 1e-6

LANES = 128
SUBLANES = 8
HEADS_PER_BLOCK = LANES // HEAD_DIM
DT_PAD = LANES
W_MAIN = 4 * D_ATTN + D_XBC + D_SSM
W_ALL = W_MAIN + DT_PAD

TM_IN = 256
TM_OUT = 256
TQ = 128
TK = 128
CHUNK = 128
LOG_WEIGHT_CUTOFF = -110.0
VMEM_LIMIT = 56 * 1024 * 1024

INTER_DTYPE = F32


def _silu(x):
    return x * jax.nn.sigmoid(x)


def _softplus(x):
    return jnp.maximum(x, 0.0) + jnp.log1p(jnp.exp(-jnp.abs(x)))


def _rms_scale(x):
    return lax.rsqrt(jnp.mean(x * x, axis=-1, keepdims=True) + NORM_EPS)


def _mod_kernel(c_ref, w_ref, b_ref, o_ref):
    c_act = _silu(c_ref[...]).astype(BF16)
    o_ref[...] = jnp.dot(c_act, w_ref[...].astype(BF16), preferred_element_type=F32) + b_ref[...]


def _modulation(c, w_ada, b_ada):
    bsz, d = c.shape
    n = w_ada.shape[1]
    tn = 1024
    return pl.pallas_call(
        _mod_kernel,
        grid=(n // tn,),
        in_specs=[pl.BlockSpec((bsz, d), lambda j: (0, 0)),
                  pl.BlockSpec((d, tn), lambda j: (0, j)),
                  pl.BlockSpec((1, tn), lambda j: (0, j))],
        out_specs=pl.BlockSpec((bsz, tn), lambda j: (0, j)),
        out_shape=jax.ShapeDtypeStruct((bsz, n), F32),
        name="adaln_mod",
    )(c, w_ada, b_ada.reshape(1, n))


def _inproj_kernel(x_ref, shift_ref, scale_ref, gain_ref, w_ref,
                   q_ref, k_ref, v_ref, za_ref, xbc_ref, zs_ref, dt_ref):
    x = x_ref[...]
    y = x * _rms_scale(x) * gain_ref[...]
    h = (y * (1.0 + scale_ref[...]) + shift_ref[...]).astype(BF16)

    def proj(lo, width):
        return jnp.dot(h, w_ref[:, lo:lo + width], preferred_element_type=F32)

    q_ref[...] = proj(0, D_ATTN).astype(q_ref.dtype)
    k_ref[...] = proj(D_ATTN, D_ATTN).astype(k_ref.dtype)
    v_ref[...] = proj(2 * D_ATTN, D_ATTN).astype(v_ref.dtype)
    za_ref[...] = proj(3 * D_ATTN, D_ATTN).astype(za_ref.dtype)
    xbc_ref[...] = proj(4 * D_ATTN, D_XBC).astype(xbc_ref.dtype)
    zs_ref[...] = proj(4 * D_ATTN + D_XBC, D_SSM).astype(zs_ref.dtype)
    dt_ref[...] = proj(W_MAIN, DT_PAD)


def _in_projection(x, mod4, gain, w_all):
    bsz, seq, d = x.shape
    row = lambda b, i: (b, i, 0)
    mod_spec = lambda j: pl.BlockSpec((None, None, 1, d), lambda b, i: (b, j, 0, 0))

    def out(width, dtype):
        return (pl.BlockSpec((None, TM_IN, width), row),
                jax.ShapeDtypeStruct((bsz, seq, width), dtype))

    outs = [out(D_ATTN, BF16), out(D_ATTN, BF16), out(D_ATTN, BF16), out(D_ATTN, INTER_DTYPE),
            out(D_XBC, INTER_DTYPE), out(D_SSM, INTER_DTYPE), out(DT_PAD, F32)]
    return pl.pallas_call(
        _inproj_kernel,
        grid=(bsz, seq // TM_IN),
        in_specs=[pl.BlockSpec((None, TM_IN, d), row),
                  mod_spec(0), mod_spec(1),
                  pl.BlockSpec((1, d), lambda b, i: (0, 0)),
                  pl.BlockSpec((d, W_ALL), lambda b, i: (0, 0), pipeline_mode=pl.Buffered(1))],
        out_specs=[o[0] for o in outs],
        out_shape=[o[1] for o in outs],
        compiler_params=pltpu.CompilerParams(
            dimension_semantics=("parallel", "parallel"), vmem_limit_bytes=VMEM_LIMIT),
        name="norm_in_proj",
    )(x, mod4, mod4, gain, w_all)


def _attn_kernel(q_ref, k_ref, v_ref, o_ref):
    qi = pl.program_id(2)
    row = lax.broadcasted_iota(jnp.int32, (TQ, TK), 0)
    col = lax.broadcasted_iota(jnp.int32, (TQ, TK), 1)
    below_diag = col < row
    tri = (row > col).astype(BF16)
    tri2 = jnp.concatenate([tri, tri], axis=0)

    outs = []
    for h in range(HEADS_PER_BLOCK):
        lanes = slice(h * HEAD_DIM, (h + 1) * HEAD_DIM)
        qh = q_ref[:, lanes] * jnp.asarray(HEAD_DIM ** -0.5, BF16)

        def block(kb, carry, acc, masked, lanes=lanes, qh=qh):
            ks = pl.multiple_of(kb * TK, TK)
            kh = k_ref[pl.ds(ks, TK), lanes]
            vh = v_ref[pl.ds(ks, TK), lanes]
            z = lax.dot_general(qh, kh, (((1,), (1,)), ((), ())), preferred_element_type=F32)
            log_om = -(jnp.maximum(z, 0.0) + jnp.log(1.0 + jnp.exp(-jnp.abs(z))))
            if masked:
                log_om = jnp.where(below_diag, log_om, 0.0)
            l_hi = log_om.astype(BF16)
            l_lo = (log_om - l_hi.astype(F32)).astype(BF16)
            tail = jnp.dot(jnp.concatenate([l_hi, l_lo], axis=1), tri2, preferred_element_type=F32)
            w = jnp.exp(z + log_om + tail + carry)
            if masked:
                w = jnp.where(below_diag, w, 0.0)
            acc = acc + jnp.dot(w.astype(BF16), vh, preferred_element_type=F32)
            carry = carry + jnp.sum(log_om, axis=1, keepdims=True)
            return carry, acc

        carry, acc = block(qi, jnp.zeros((TQ, 1), F32), jnp.zeros((TQ, HEAD_DIM), F32), True)

        def keep_going(kb, carry):
            return jnp.logical_and(kb >= 0, jnp.max(carry) > LOG_WEIGHT_CUTOFF)

        def body(state, block=block):
            kb, carry, acc, _ = state
            carry, acc = block(kb, carry, acc, False)
            return kb - 1, carry, acc, keep_going(kb - 1, carry)

        _, _, acc, _ = lax.while_loop(lambda s: s[3], body,
                                      (qi - 1, carry, acc, keep_going(qi - 1, carry)))
        outs.append(acc)
    o_ref[...] = jnp.concatenate(outs, axis=1).astype(o_ref.dtype)


def _attention(q, k, v):
    bsz, seq, d = q.shape
    blocks = d // LANES
    return pl.pallas_call(
        _attn_kernel,
        grid=(bsz, blocks, seq // TQ),
        in_specs=[pl.BlockSpec((None, TQ, LANES), lambda b, p, i: (b, i, p)),
                  pl.BlockSpec((None, seq, LANES), lambda b, p, i: (b, 0, p)),
                  pl.BlockSpec((None, seq, LANES), lambda b, p, i: (b, 0, p))],
        out_specs=pl.BlockSpec((None, TQ, LANES), lambda b, p, i: (b, i, p)),
        out_shape=jax.ShapeDtypeStruct((bsz, seq, d), INTER_DTYPE),
        compiler_params=pltpu.CompilerParams(
            dimension_semantics=("parallel", "parallel", "parallel")),
        name="stickbreak_attn",
    )(q, k, v)


def _expand_heads(a):
    left = lax.broadcasted_iota(jnp.int32, (a.shape[0], LANES), 1) < HEAD_DIM
    pieces = []
    for p in range(SSM_HEADS // HEADS_PER_BLOCK):
        h0 = p * HEADS_PER_BLOCK
        pieces.append(jnp.where(left, a[:, h0:h0 + 1], a[:, h0 + 1:h0 + 2]))
    return jnp.concatenate(pieces, axis=1)


def _ssd_kernel(xbc_ref, dt_ref, zs_ref, cw_ref, cb_ref, dtb_ref, alog_ref, dskip_ref, gain_ref,
                y_ref, state_ref, hist_ref):
    @pl.when(pl.program_id(1) == 0)
    def _():
        state_ref[...] = jnp.zeros_like(state_ref)
        hist_ref[...] = jnp.zeros_like(hist_ref)

    u = xbc_ref[...].astype(F32)
    ext = jnp.concatenate([hist_ref[...], u], axis=0)
    hist_ref[...] = u[CHUNK - SUBLANES:, :]
    conv = cb_ref[...]
    for kk in range(CONV_WIDTH):
        off = SUBLANES - (CONV_WIDTH - 1) + kk
        conv = conv + cw_ref[kk:kk + 1, :] * ext[off:off + CHUNK, :]
    xa = _silu(conv)
    xs = xa[:, :D_SSM]
    b_in = xa[:, D_SSM:D_SSM + SSM_GROUPS * SSM_STATE]
    c_in = xa[:, D_SSM + SSM_GROUPS * SSM_STATE:]

    dt = _softplus(dt_ref[...] + dtb_ref[...])
    log_decay = dt * (-jnp.exp(alog_ref[...]))
    row = lax.broadcasted_iota(jnp.int32, (CHUNK, CHUNK), 0)
    col = lax.broadcasted_iota(jnp.int32, (CHUNK, CHUNK), 1)
    causal = col <= row
    tri_inc = causal.astype(BF16)
    p1 = log_decay.astype(BF16)
    r1 = log_decay - p1.astype(F32)
    p2 = r1.astype(BF16)
    p3 = (r1 - p2.astype(F32)).astype(BF16)
    a_cum = (jnp.dot(tri_inc, p1, preferred_element_type=F32)
             + jnp.dot(tri_inc, p2, preferred_element_type=F32)
             + jnp.dot(tri_inc, p3, preferred_element_type=F32))
    a_cum_t = a_cum.T
    a_last = a_cum[CHUNK - 1:CHUNK, :]
    exp_a = _expand_heads(jnp.exp(a_cum))
    to_end = _expand_heads(jnp.exp(a_last - a_cum))
    xdt = xs * _expand_heads(dt)
    xdt_b = xdt.astype(BF16)
    xdt_end_b = (xdt * to_end).astype(BF16)

    left = lax.broadcasted_iota(jnp.int32, (CHUNK, LANES), 1) < HEAD_DIM
    gw = (SSM_HEADS // SSM_GROUPS) * HEAD_DIM
    y_diag, y_off = [], []
    for g in range(SSM_GROUPS):
        bg_t = b_in[:, g * SSM_STATE:(g + 1) * SSM_STATE].T.astype(BF16)
        cg = c_in[:, g * SSM_STATE:(g + 1) * SSM_STATE].astype(BF16)
        cb = jnp.dot(cg, bg_t, preferred_element_type=F32)
        cols = slice(g * gw, (g + 1) * gw)
        prev = state_ref[:, cols]
        y_off.append(jnp.dot(cg, prev.astype(BF16), preferred_element_type=F32) * exp_a[:, cols])
        new = jnp.dot(bg_t, xdt_end_b[:, cols], preferred_element_type=F32)
        state_ref[:, cols] = prev * exp_a[CHUNK - 1:CHUNK, cols] + new
        for p in range(SSM_HEADS // SSM_GROUPS // HEADS_PER_BLOCK):
            h0 = g * (SSM_HEADS // SSM_GROUPS) + p * HEADS_PER_BLOCK
            m = []
            for h in (h0, h0 + 1):
                seg = a_cum[:, h:h + 1] - a_cum_t[h:h + 1, :]
                m.append((cb * jnp.exp(jnp.where(causal, seg, -1e30))).astype(BF16))
            xp = xdt_b[:, h0 * HEAD_DIM:(h0 + HEADS_PER_BLOCK) * HEAD_DIM]
            zero = jnp.zeros_like(xp)
            rhs = jnp.concatenate([jnp.where(left, xp, zero), jnp.where(left, zero, xp)], axis=0)
            y_diag.append(jnp.dot(jnp.concatenate(m, axis=1), rhs, preferred_element_type=F32))

    y = jnp.concatenate(y_diag, axis=1) + jnp.concatenate(y_off, axis=1) + xs * dskip_ref[...]
    gated = y * _silu(zs_ref[...].astype(F32))
    y_ref[...] = (gated * _rms_scale(gated) * gain_ref[...]).astype(y_ref.dtype)


def _ssd(xbc, dt_raw, z_ssm, conv_w, conv_b, dt_bias, a_log, d_skip, gain):
    bsz, seq, _ = xbc.shape
    row = lambda b, c: (b, c, 0)
    const = lambda shape: pl.BlockSpec(shape, lambda b, c: (0, 0))
    pad = lambda a: jnp.pad(a, (0, DT_PAD - SSM_HEADS)).reshape(1, DT_PAD)
    return pl.pallas_call(
        _ssd_kernel,
        grid=(bsz, seq // CHUNK),
        in_specs=[pl.BlockSpec((None, CHUNK, D_XBC), row),
                  pl.BlockSpec((None, CHUNK, DT_PAD), row),
                  pl.BlockSpec((None, CHUNK, D_SSM), row),
                  const((CONV_WIDTH, D_XBC)), const((1, D_XBC)),
                  const((1, DT_PAD)), const((1, DT_PAD)),
                  const((1, D_SSM)), const((1, D_SSM))],
        out_specs=pl.BlockSpec((None, CHUNK, D_SSM), row),
        out_shape=jax.ShapeDtypeStruct((bsz, seq, D_SSM), BF16),
        scratch_shapes=[pltpu.VMEM((SSM_STATE, D_SSM), F32),
                        pltpu.VMEM((SUBLANES, D_XBC), F32)],
        compiler_params=pltpu.CompilerParams(dimension_semantics=("parallel", "arbitrary")),
        name="conv_ssd_norm",
    )(xbc, dt_raw, z_ssm, conv_w, conv_b.reshape(1, D_XBC), pad(dt_bias), pad(a_log),
      jnp.repeat(d_skip, HEAD_DIM).reshape(1, D_SSM), gain.reshape(1, D_SSM))


def _out_kernel(o_ref, za_ref, ys_ref, x_ref, gate_ref, sbg_ref, nfg_ref, w_ref, out_ref):
    o = o_ref[...].astype(F32)
    y_attn = (o * _rms_scale(o) * sbg_ref[...]) * _silu(za_ref[...].astype(F32))
    mixed = (jnp.dot(y_attn.astype(BF16), w_ref[:D_ATTN, :], preferred_element_type=F32)
             + jnp.dot(ys_ref[...], w_ref[D_ATTN:, :], preferred_element_type=F32))
    xo = x_ref[...] + gate_ref[...] * mixed
    out_ref[...] = xo * _rms_scale(xo) * nfg_ref[...]


def _out_projection(o, z_attn, y_ssm, x, mod4, sb_gain, nf_gain, w_out):
    bsz, seq, d = x.shape
    row = lambda b, i: (b, i, 0)
    const = lambda shape: pl.BlockSpec(shape, lambda b, i: (0, 0))
    return pl.pallas_call(
        _out_kernel,
        grid=(bsz, seq // TM_OUT),
        in_specs=[pl.BlockSpec((None, TM_OUT, D_ATTN), row),
                  pl.BlockSpec((None, TM_OUT, D_ATTN), row),
                  pl.BlockSpec((None, TM_OUT, D_SSM), row),
                  pl.BlockSpec((None, TM_OUT, d), row),
                  pl.BlockSpec((None, None, 1, d), lambda b, i: (b, 2, 0, 0)),
                  const((1, D_ATTN)), const((1, d)),
                  const((D_ATTN + D_SSM, d))],
        out_specs=pl.BlockSpec((None, TM_OUT, d), row),
        out_shape=jax.ShapeDtypeStruct((bsz, seq, d), F32),
        compiler_params=pltpu.CompilerParams(
            dimension_semantics=("parallel", "parallel"), vmem_limit_bytes=VMEM_LIMIT),
        name="out_proj_norm",
    )(o, z_attn, y_ssm, x, mod4, sb_gain, nf_gain, w_out)


def kernel(x, c, w_ada, b_ada, norm_in_gain, w_in, conv_w, conv_b, dt_bias, a_log, d_skip,
           sb_norm_gain, ssm_norm_gain, w_out, norm_f_gain):
    bsz, seq, d = x.shape
    depth = w_in.shape[0]
    n_dt = SSM_HEADS
    for layer in range(depth):
        mod = _modulation(c, w_ada[layer], b_ada[layer])
        mod4 = mod.reshape(bsz, 3, 1, d)
        w = w_in[layer]
        w_all = jnp.concatenate(
            [w[:, :4 * D_ATTN + D_XBC], w[:, 4 * D_ATTN + D_XBC + n_dt:],
             w[:, 4 * D_ATTN + D_XBC:4 * D_ATTN + D_XBC + n_dt],
             jnp.zeros((d, DT_PAD - n_dt), w.dtype)], axis=1).astype(BF16)
        q, k, v, z_attn, xbc, z_ssm, dt_raw = _in_projection(
            x, mod4, norm_in_gain[layer].reshape(1, d), w_all)
        o = _attention(q, k, v)
        y_ssm = _ssd(xbc, dt_raw, z_ssm, conv_w[layer], conv_b[layer], dt_bias[layer],
                     a_log[layer], d_skip[layer], ssm_norm_gain[layer])
        last = layer == depth - 1
        assert last, "final norm is fused into the last layer's out-projection"
        x = _out_projection(o, z_attn, y_ssm, x, mod4, sb_norm_gain[layer].reshape(1, D_ATTN),
                            norm_f_gain.reshape(1, d), w_out[layer].astype(BF16))
    return x
```

```python
import functools

import jax
import jax.numpy as jnp
from jax import lax
from jax.experimental import pallas as pl
from jax.experimental.pallas import tpu as pltpu

F32 = jnp.float32
BF16 = jnp.bfloat16

D_MODEL = 1024
D_ATTN = 1024
D_SSM = 1024
HEAD_DIM = 64
SSM_HEADS = 16
SSM_GROUPS = 2
SSM_STATE = 128
CONV_WIDTH = 4
D_XBC = D_SSM + 2 * SSM_GROUPS * SSM_STATE
NORM_EPS = 1e-6

LANES = 128
SUBLANES = 8
HEADS_PER_BLOCK = LANES // HEAD_DIM
DT_PAD = LANES
W_MAIN = 4 * D_ATTN + D_XBC + D_SSM
W_ALL = W_MAIN + DT_PAD

TM_IN = 256
TM_OUT = 256
TQ = 128
TK = 128
ATTN_LANES = 512
HEAD_BLOCKS = 3
CHUNK = 128
LOG2_E = 1.4426950408889634
LOG2_WEIGHT_CUTOFF = -160.0
VMEM_LIMIT = 56 * 1024 * 1024

INTER_DTYPE = F32


def _silu(x):
    return x * jax.nn.sigmoid(x)


def _softplus(x):
    return jnp.maximum(x, 0.0) + jnp.log1p(jnp.exp(-jnp.abs(x)))


def _rms_scale(x):
    return lax.rsqrt(jnp.mean(x * x, axis=-1, keepdims=True) + NORM_EPS)


def _mod_kernel(c_ref, w_ref, b_ref, o_ref):
    c_act = _silu(c_ref[...]).astype(BF16)
    o_ref[...] = jnp.dot(c_act, w_ref[...].astype(BF16), preferred_element_type=F32) + b_ref[...]


def _modulation(c, w_ada, b_ada):
    bsz, d = c.shape
    n = w_ada.shape[1]
    tn = 1024
    return pl.pallas_call(
        _mod_kernel,
        grid=(n // tn,),
        in_specs=[pl.BlockSpec((bsz, d), lambda j: (0, 0)),
                  pl.BlockSpec((d, tn), lambda j: (0, j)),
                  pl.BlockSpec((1, tn), lambda j: (0, j))],
        out_specs=pl.BlockSpec((bsz, tn), lambda j: (0, j)),
        out_shape=jax.ShapeDtypeStruct((bsz, n), F32),
        name="adaln_mod",
    )(c, w_ada, b_ada.reshape(1, n))


def _inproj_kernel(x_ref, shift_ref, scale_ref, gain_ref, w_ref,
                   q_ref, k_ref, v_ref, za_ref, xbc_ref, zs_ref, dt_ref):
    x = x_ref[...]
    y = x * _rms_scale(x) * gain_ref[...]
    h = (y * (1.0 + scale_ref[...]) + shift_ref[...]).astype(BF16)

    def proj(lo, width):
        return jnp.dot(h, w_ref[:, lo:lo + width], preferred_element_type=F32)

    q_ref[...] = proj(0, D_ATTN).astype(q_ref.dtype)
    k_ref[...] = proj(D_ATTN, D_ATTN).astype(k_ref.dtype)
    v_ref[...] = proj(2 * D_ATTN, D_ATTN).astype(v_ref.dtype)
    za_ref[...] = proj(3 * D_ATTN, D_ATTN).astype(za_ref.dtype)
    xbc_ref[...] = proj(4 * D_ATTN, D_XBC).astype(xbc_ref.dtype)
    zs_ref[...] = proj(4 * D_ATTN + D_XBC, D_SSM).astype(zs_ref.dtype)
    dt_ref[...] = proj(W_MAIN, DT_PAD)


def _in_projection(x, mod4, gain, w_all):
    bsz, seq, d = x.shape
    row = lambda b, i: (b, i, 0)
    mod_spec = lambda j: pl.BlockSpec((None, None, 1, d), lambda b, i: (b, j, 0, 0))

    def out(width, dtype):
        return (pl.BlockSpec((None, TM_IN, width), row),
                jax.ShapeDtypeStruct((bsz, seq, width), dtype))

    outs = [out(D_ATTN, BF16), out(D_ATTN, BF16), out(D_ATTN, BF16), out(D_ATTN, INTER_DTYPE),
            out(D_XBC, INTER_DTYPE), out(D_SSM, INTER_DTYPE), out(DT_PAD, F32)]
    return pl.pallas_call(
        _inproj_kernel,
        grid=(bsz, seq // TM_IN),
        in_specs=[pl.BlockSpec((None, TM_IN, d), row),
                  mod_spec(0), mod_spec(1),
                  pl.BlockSpec((1, d), lambda b, i: (0, 0)),
                  pl.BlockSpec((d, W_ALL), lambda b, i: (0, 0), pipeline_mode=pl.Buffered(1))],
        out_specs=[o[0] for o in outs],
        out_shape=[o[1] for o in outs],
        compiler_params=pltpu.CompilerParams(
            dimension_semantics=("parallel", "parallel"), vmem_limit_bytes=VMEM_LIMIT),
        name="norm_in_proj",
    )(x, mod4, mod4, gain, w_all)


def _attn_kernel(q_ref, k_ref, v_ref, o_ref, acc_ref, carry_ref, worst_ref):
    qi = pl.program_id(2)
    row = lax.broadcasted_iota(jnp.int32, (TQ, TK), 0)
    col = lax.broadcasted_iota(jnp.int32, (TQ, TK), 1)
    below = col < row
    below2 = jnp.concatenate([below, below], axis=1)
    tri = (row > col).astype(BF16)
    zero = jnp.zeros_like(tri)
    tri_bd = jnp.concatenate([jnp.concatenate([tri, zero], axis=1),
                              jnp.concatenate([zero, tri], axis=1)], axis=0)
    left = lax.broadcasted_iota(jnp.int32, (TK, LANES), 1) < HEAD_DIM

    def split_heads(a):
        z0 = jnp.zeros_like(a)
        return jnp.concatenate([jnp.where(left, a, z0), jnp.where(left, z0, a)], axis=0)

    pairs = range(ATTN_LANES // LANES)

    def scores(kb, p, diagonal):
        ks = pl.multiple_of(kb * TK, TK)
        ls = slice(p * LANES, (p + 1) * LANES)
        kk = split_heads(k_ref[pl.ds(ks, TK), ls])
        qk = lax.dot_general(q_ref[:, ls], kk, (((1,), (1,)), ((), ())),
                             preferred_element_type=F32)
        nt = qk * (-LOG2_E * HEAD_DIM ** -0.5)
        l2_om = jnp.minimum(nt, 0.0) - jnp.log2(1.0 + jnp.exp2(-jnp.abs(nt)))
        if diagonal:
            l2_om = jnp.where(below2, l2_om, 0.0)
        l_b = l2_om.astype(BF16)
        tail = jnp.dot(l_b, tri_bd, preferred_element_type=F32)
        l2_w = (l2_om - nt) + tail
        sums = [tail[:, c:c + 1] + l_b[:, c:c + 1].astype(F32) for c in (0, TK)]
        return l2_w, sums

    def weighted_values(kb, p, l2_w, carry, diagonal):
        ks = pl.multiple_of(kb * TK, TK)
        if carry is not None:
            l2_w = l2_w + jnp.concatenate([jnp.broadcast_to(carry[0], (TQ, TK)),
                                           jnp.broadcast_to(carry[1], (TQ, TK))], axis=1)
        w = jnp.exp2(l2_w)
        if diagonal:
            w = jnp.where(below2, w, 0.0)
        vv = split_heads(v_ref[pl.ds(ks, TK), p * LANES:(p + 1) * LANES])
        return jnp.dot(w.astype(BF16), vv, preferred_element_type=F32)

    def sweep(n_blocks, kb0, resume):
        per_block = [[scores(kb0 - j, p, diagonal=(not resume and j == 0)) for p in pairs]
                     for j in range(n_blocks)]
        worst = None
        for p in pairs:
            ls = slice(p * LANES, (p + 1) * LANES)
            carry = [carry_ref[2 * p], carry_ref[2 * p + 1]] if resume else None
            acc = acc_ref[:, ls] if resume else None
            for j in range(n_blocks):
                l2_w, sums = per_block[j][p]
                pv = weighted_values(kb0 - j, p, l2_w, carry, diagonal=(not resume and j == 0))
                acc = pv if acc is None else acc + pv
                carry = sums if carry is None else [carry[0] + sums[0], carry[1] + sums[1]]
            acc_ref[:, ls] = acc
            carry_ref[2 * p] = carry[0]
            carry_ref[2 * p + 1] = carry[1]
            m = jnp.maximum(carry[0], carry[1])
            worst = m if worst is None else jnp.maximum(worst, m)
        return worst

    for n in range(1, HEAD_BLOCKS + 1):
        cond = (qi == n - 1) if n < HEAD_BLOCKS else (qi >= n - 1)

        @pl.when(cond)
        def _(n=n):
            worst_ref[...] = sweep(n, qi, resume=False)

    def keep_going(kb, worst):
        return jnp.logical_and(kb >= 0, worst > LOG2_WEIGHT_CUTOFF)

    def body(state):
        kb, _ = state
        return kb - 1, keep_going(kb - 1, jnp.max(sweep(1, kb, resume=True)))

    kb = qi - HEAD_BLOCKS
    lax.while_loop(lambda s: s[1], body, (kb, keep_going(kb, jnp.max(worst_ref[...]))))
    o_ref[...] = acc_ref[...].astype(o_ref.dtype)


def _attention(q, k, v):
    bsz, seq, d = q.shape
    return pl.pallas_call(
        _attn_kernel,
        grid=(bsz, d // ATTN_LANES, seq // TQ),
        in_specs=[pl.BlockSpec((None, TQ, ATTN_LANES), lambda b, p, i: (b, i, p)),
                  pl.BlockSpec((None, seq, ATTN_LANES), lambda b, p, i: (b, 0, p)),
                  pl.BlockSpec((None, seq, ATTN_LANES), lambda b, p, i: (b, 0, p))],
        out_specs=pl.BlockSpec((None, TQ, ATTN_LANES), lambda b, p, i: (b, i, p)),
        out_shape=jax.ShapeDtypeStruct((bsz, seq, d), INTER_DTYPE),
        scratch_shapes=[pltpu.VMEM((TQ, ATTN_LANES), F32),
                        pltpu.VMEM((ATTN_LANES // HEAD_DIM, TQ, 1), F32),
                        pltpu.VMEM((TQ, 1), F32)],
        compiler_params=pltpu.CompilerParams(
            dimension_semantics=("parallel", "parallel", "parallel"),
            vmem_limit_bytes=VMEM_LIMIT),
        name="stickbreak_attn",
    )(q, k, v)


def _expand_heads(a):
    left = lax.broadcasted_iota(jnp.int32, (a.shape[0], LANES), 1) < HEAD_DIM
    pieces = []
    for p in range(SSM_HEADS // HEADS_PER_BLOCK):
        h0 = p * HEADS_PER_BLOCK
        pieces.append(jnp.where(left, a[:, h0:h0 + 1], a[:, h0 + 1:h0 + 2]))
    return jnp.concatenate(pieces, axis=1)


def _ssd_kernel(xbc_ref, dt_ref, zs_ref, cw_ref, cb_ref, dtb_ref, alog_ref, dskip_ref, gain_ref,
                y_ref, state_ref, hist_ref):
    @pl.when(pl.program_id(1) == 0)
    def _():
        state_ref[...] = jnp.zeros_like(state_ref)
        hist_ref[...] = jnp.zeros_like(hist_ref)

    u = xbc_ref[...].astype(F32)
    ext = jnp.concatenate([hist_ref[...], u], axis=0)
    hist_ref[...] = u[CHUNK - SUBLANES:, :]
    conv = cb_ref[...] + cw_ref[CONV_WIDTH - 1:CONV_WIDTH, :] * u
    for kk in range(CONV_WIDTH - 1):
        off = SUBLANES - (CONV_WIDTH - 1) + kk
        shifted = pltpu.roll(ext, ext.shape[0] - off, axis=0)[:CHUNK, :]
        conv = conv + cw_ref[kk:kk + 1, :] * shifted
    xa = _silu(conv)
    xs = xa[:, :D_SSM]
    b_in = xa[:, D_SSM:D_SSM + SSM_GROUPS * SSM_STATE]
    c_in = xa[:, D_SSM + SSM_GROUPS * SSM_STATE:]

    dt = _softplus(dt_ref[...] + dtb_ref[...])
    log_decay = dt * (-jnp.exp(alog_ref[...]))
    row = lax.broadcasted_iota(jnp.int32, (CHUNK, CHUNK), 0)
    col = lax.broadcasted_iota(jnp.int32, (CHUNK, CHUNK), 1)
    causal = col <= row
    tri_inc = causal.astype(BF16)
    p1 = log_decay.astype(BF16)
    r1 = log_decay - p1.astype(F32)
    p2 = r1.astype(BF16)
    p3 = (r1 - p2.astype(F32)).astype(BF16)
    a_cum = (jnp.dot(tri_inc, p1, preferred_element_type=F32)
             + jnp.dot(tri_inc, p2, preferred_element_type=F32)
             + jnp.dot(tri_inc, p3, preferred_element_type=F32))
    a_cum_t = a_cum.T
    a_last = a_cum[CHUNK - 1:CHUNK, :]
    exp_a = _expand_heads(jnp.exp(a_cum))
    to_end = _expand_heads(jnp.exp(a_last - a_cum))
    xdt = xs * _expand_heads(dt)
    xdt_b = xdt.astype(BF16)
    xdt_end_b = (xdt * to_end).astype(BF16)

    left = lax.broadcasted_iota(jnp.int32, (CHUNK, LANES), 1) < HEAD_DIM
    gw = (SSM_HEADS // SSM_GROUPS) * HEAD_DIM
    y_diag, y_off = [], []
    for g in range(SSM_GROUPS):
        bg_t = b_in[:, g * SSM_STATE:(g + 1) * SSM_STATE].T.astype(BF16)
        cg = c_in[:, g * SSM_STATE:(g + 1) * SSM_STATE].astype(BF16)
        cb = jnp.dot(cg, bg_t, preferred_element_type=F32)
        cols = slice(g * gw, (g + 1) * gw)
        prev = state_ref[:, cols]
        y_off.append(jnp.dot(cg, prev.astype(BF16), preferred_element_type=F32) * exp_a[:, cols])
        new = jnp.dot(bg_t, xdt_end_b[:, cols], preferred_element_type=F32)
        state_ref[:, cols] = prev * exp_a[CHUNK - 1:CHUNK, cols] + new
        for p in range(SSM_HEADS // SSM_GROUPS // HEADS_PER_BLOCK):
            h0 = g * (SSM_HEADS // SSM_GROUPS) + p * HEADS_PER_BLOCK
            m = []
            for h in (h0, h0 + 1):
                seg = a_cum[:, h:h + 1] - a_cum_t[h:h + 1, :]
                m.append((cb * jnp.exp(jnp.where(causal, seg, -1e30))).astype(BF16))
            xp = xdt_b[:, h0 * HEAD_DIM:(h0 + HEADS_PER_BLOCK) * HEAD_DIM]
            zero = jnp.zeros_like(xp)
            rhs = jnp.concatenate([jnp.where(left, xp, zero), jnp.where(left, zero, xp)], axis=0)
            y_diag.append(jnp.dot(jnp.concatenate(m, axis=1), rhs, preferred_element_type=F32))

    y = jnp.concatenate(y_diag, axis=1) + jnp.concatenate(y_off, axis=1) + xs * dskip_ref[...]
    gated = y * _silu(zs_ref[...].astype(F32))
    y_ref[...] = (gated * _rms_scale(gated) * gain_ref[...]).astype(y_ref.dtype)


def _ssd(xbc, dt_raw, z_ssm, conv_w, conv_b, dt_bias, a_log, d_skip, gain):
    bsz, seq, _ = xbc.shape
    row = lambda b, c: (b, c, 0)
    const = lambda shape: pl.BlockSpec(shape, lambda b, c: (0, 0))
    pad = lambda a: jnp.pad(a, (0, DT_PAD - SSM_HEADS)).reshape(1, DT_PAD)
    return pl.pallas_call(
        _ssd_kernel,
        grid=(bsz, seq // CHUNK),
        in_specs=[pl.BlockSpec((None, CHUNK, D_XBC), row),
                  pl.BlockSpec((None, CHUNK, DT_PAD), row),
                  pl.BlockSpec((None, CHUNK, D_SSM), row),
                  const((CONV_WIDTH, D_XBC)), const((1, D_XBC)),
                  const((1, DT_PAD)), const((1, DT_PAD)),
                  const((1, D_SSM)), const((1, D_SSM))],
        out_specs=pl.BlockSpec((None, CHUNK, D_SSM), row),
        out_shape=jax.ShapeDtypeStruct((bsz, seq, D_SSM), BF16),
        scratch_shapes=[pltpu.VMEM((SSM_STATE, D_SSM), F32),
                        pltpu.VMEM((SUBLANES, D_XBC), F32)],
        compiler_params=pltpu.CompilerParams(dimension_semantics=("parallel", "arbitrary")),
        name="conv_ssd_norm",
    )(xbc, dt_raw, z_ssm, conv_w, conv_b.reshape(1, D_XBC), pad(dt_bias), pad(a_log),
      jnp.repeat(d_skip, HEAD_DIM).reshape(1, D_SSM), gain.reshape(1, D_SSM))


def _out_kernel(o_ref, za_ref, ys_ref, x_ref, gate_ref, sbg_ref, nfg_ref, w_ref, out_ref):
    o = o_ref[...].astype(F32)
    y_attn = (o * _rms_scale(o) * sbg_ref[...]) * _silu(za_ref[...].astype(F32))
    mixed = (jnp.dot(y_attn.astype(BF16), w_ref[:D_ATTN, :], preferred_element_type=F32)
             + jnp.dot(ys_ref[...], w_ref[D_ATTN:, :], preferred_element_type=F32))
    xo = x_ref[...] + gate_ref[...] * mixed
    out_ref[...] = xo * _rms_scale(xo) * nfg_ref[...]


def _out_projection(o, z_attn, y_ssm, x, mod4, sb_gain, nf_gain, w_out):
    bsz, seq, d = x.shape
    row = lambda b, i: (b, i, 0)
    const = lambda shape: pl.BlockSpec(shape, lambda b, i: (0, 0))
    return pl.pallas_call(
        _out_kernel,
        grid=(bsz, seq // TM_OUT),
        in_specs=[pl.BlockSpec((None, TM_OUT, D_ATTN), row),
                  pl.BlockSpec((None, TM_OUT, D_ATTN), row),
                  pl.BlockSpec((None, TM_OUT, D_SSM), row),
                  pl.BlockSpec((None, TM_OUT, d), row),
                  pl.BlockSpec((None, None, 1, d), lambda b, i: (b, 2, 0, 0)),
                  const((1, D_ATTN)), const((1, d)),
                  const((D_ATTN + D_SSM, d))],
        out_specs=pl.BlockSpec((None, TM_OUT, d), row),
        out_shape=jax.ShapeDtypeStruct((bsz, seq, d), F32),
        compiler_params=pltpu.CompilerParams(
            dimension_semantics=("parallel", "parallel"), vmem_limit_bytes=VMEM_LIMIT),
        name="out_proj_norm",
    )(o, z_attn, y_ssm, x, mod4, sb_gain, nf_gain, w_out)


def kernel(x, c, w_ada, b_ada, norm_in_gain, w_in, conv_w, conv_b, dt_bias, a_log, d_skip,
           sb_norm_gain, ssm_norm_gain, w_out, norm_f_gain):
    bsz, seq, d = x.shape
    depth = w_in.shape[0]
    n_dt = SSM_HEADS
    for layer in range(depth):
        mod = _modulation(c, w_ada[layer], b_ada[layer])
        mod4 = mod.reshape(bsz, 3, 1, d)
        w = w_in[layer]
        w_all = jnp.concatenate(
            [w[:, :4 * D_ATTN + D_XBC], w[:, 4 * D_ATTN + D_XBC + n_dt:],
             w[:, 4 * D_ATTN + D_XBC:4 * D_ATTN + D_XBC + n_dt],
             jnp.zeros((d, DT_PAD - n_dt), w.dtype)], axis=1).astype(BF16)
        q, k, v, z_attn, xbc, z_ssm, dt_raw = _in_projection(
            x, mod4, norm_in_gain[layer].reshape(1, d), w_all)
        o = _attention(q, k, v)
        y_ssm = _ssd(xbc, dt_raw, z_ssm, conv_w[layer], conv_b[layer], dt_bias[layer],
                     a_log[layer], d_skip[layer], ssm_norm_gain[layer])
        last = layer == depth - 1
        assert last, "final norm is fused into the last layer's out-projection"
        x = _out_projection(o, z_attn, y_ssm, x, mod4, sb_norm_gain[layer].reshape(1, D_ATTN),
                            norm_f_gain.reshape(1, d), w_out[layer].astype(BF16))
    return x
```

```python
import functools

import jax
import jax.numpy as jnp
from jax import lax
from jax.experimental import pallas as pl
from jax.experimental.pallas import tpu as pltpu

F32 = jnp.float32
BF16 = jnp.bfloat16

D_MODEL = 1024
D_ATTN = 1024
D_SSM = 1024
HEAD_DIM = 64
SSM_HEADS = 16
SSM_GROUPS = 2
SSM_STATE = 128
CONV_WIDTH = 4
D_XBC = D_SSM + 2 * SSM_GROUPS * SSM_STATE
NORM_EPS = 1e-6

LANES = 128
SUBLANES = 8
HEADS_PER_BLOCK = LANES // HEAD_DIM
DT_PAD = LANES
W_MAIN = 4 * D_ATTN + D_XBC + D_SSM
W_ALL = W_MAIN + DT_PAD

TM_IN = 512
TM_OUT = 512
TQ = 128
TK = 128
ATTN_LANES = 512
HEAD_BLOCKS = 5
CHUNK = 128
LOG2_E = 1.4426950408889634
LOG2_WEIGHT_CUTOFF = -127.0
VMEM_LIMIT = 56 * 1024 * 1024

INTER_DTYPE = BF16


def _silu(x):
    return x * jax.nn.sigmoid(x)


def _softplus(x):
    return jnp.maximum(x, 0.0) + jnp.log1p(jnp.exp(-jnp.abs(x)))


def _rms_scale(x):
    return lax.rsqrt(jnp.mean(x * x, axis=-1, keepdims=True) + NORM_EPS)


def _mod_kernel(c_ref, w_ref, b_ref, o_ref):
    c_act = _silu(c_ref[...]).astype(BF16)
    o_ref[...] = jnp.dot(c_act, w_ref[...].astype(BF16), preferred_element_type=F32) + b_ref[...]


def _modulation(c, w_ada, b_ada):
    bsz, d = c.shape
    n = w_ada.shape[1]
    tn = 1024
    return pl.pallas_call(
        _mod_kernel,
        grid=(n // tn,),
        in_specs=[pl.BlockSpec((bsz, d), lambda j: (0, 0)),
                  pl.BlockSpec((d, tn), lambda j: (0, j)),
                  pl.BlockSpec((1, tn), lambda j: (0, j))],
        out_specs=pl.BlockSpec((bsz, tn), lambda j: (0, j)),
        out_shape=jax.ShapeDtypeStruct((bsz, n), F32),
        name="adaln_mod",
    )(c, w_ada, b_ada.reshape(1, n))


def _inproj_kernel(x_ref, shift_ref, scale_ref, gain_ref, w_ref,
                   q_ref, k_ref, v_ref, za_ref, xbc_ref, zs_ref, dt_ref):
    x = x_ref[...]
    y = x * _rms_scale(x) * gain_ref[...]
    h = (y * (1.0 + scale_ref[...]) + shift_ref[...]).astype(BF16)

    def proj(lo, width):
        return jnp.dot(h, w_ref[:, lo:lo + width], preferred_element_type=F32)

    q_ref[...] = proj(0, D_ATTN).astype(q_ref.dtype)
    k_ref[...] = proj(D_ATTN, D_ATTN).astype(k_ref.dtype)
    v_ref[...] = proj(2 * D_ATTN, D_ATTN).astype(v_ref.dtype)
    za_ref[...] = proj(3 * D_ATTN, D_ATTN).astype(za_ref.dtype)
    xbc_ref[...] = proj(4 * D_ATTN, D_XBC).astype(xbc_ref.dtype)
    zs_ref[...] = proj(4 * D_ATTN + D_XBC, D_SSM).astype(zs_ref.dtype)
    dt_ref[...] = proj(W_MAIN, DT_PAD)


def _in_projection(x, mod4, gain, w_all):
    bsz, seq, d = x.shape
    row = lambda b, i: (b, i, 0)
    mod_spec = lambda j: pl.BlockSpec((None, None, 1, d), lambda b, i: (b, j, 0, 0))

    def out(width, dtype):
        return (pl.BlockSpec((None, TM_IN, width), row),
                jax.ShapeDtypeStruct((bsz, seq, width), dtype))

    outs = [out(D_ATTN, BF16), out(D_ATTN, BF16), out(D_ATTN, BF16), out(D_ATTN, INTER_DTYPE),
            out(D_XBC, INTER_DTYPE), out(D_SSM, INTER_DTYPE), out(DT_PAD, F32)]
    return pl.pallas_call(
        _inproj_kernel,
        grid=(bsz, seq // TM_IN),
        in_specs=[pl.BlockSpec((None, TM_IN, d), row),
                  mod_spec(0), mod_spec(1),
                  pl.BlockSpec((1, d), lambda b, i: (0, 0)),
                  pl.BlockSpec((d, W_ALL), lambda b, i: (0, 0), pipeline_mode=pl.Buffered(1))],
        out_specs=[o[0] for o in outs],
        out_shape=[o[1] for o in outs],
        compiler_params=pltpu.CompilerParams(
            dimension_semantics=("parallel", "parallel"), vmem_limit_bytes=VMEM_LIMIT),
        name="norm_in_proj",
    )(x, mod4, mod4, gain, w_all)


def _attn_kernel(q_ref, k_ref, v_ref, o_ref, acc_ref, carry_ref, worst_ref):
    qi = pl.program_id(2)
    row = lax.broadcasted_iota(jnp.int32, (TQ, TK), 0)
    col = lax.broadcasted_iota(jnp.int32, (TQ, TK), 1)
    below = col < row
    below2 = jnp.concatenate([below, below], axis=1)
    def block_diag(a):
        zero = jnp.zeros_like(a)
        return jnp.concatenate([jnp.concatenate([a, zero], axis=1),
                                jnp.concatenate([zero, a], axis=1)], axis=0)

    tri_bd = block_diag((row > col).astype(BF16))
    left = lax.broadcasted_iota(jnp.int32, (TK, LANES), 1) < HEAD_DIM

    def split_heads(a):
        z0 = jnp.zeros_like(a)
        return jnp.concatenate([jnp.where(left, a, z0), jnp.where(left, z0, a)], axis=0)

    pairs = range(ATTN_LANES // LANES)

    def scores(kb, p, diagonal):
        ks = pl.multiple_of(kb * TK, TK)
        ls = slice(p * LANES, (p + 1) * LANES)
        kk = split_heads(k_ref[pl.ds(ks, TK), ls])
        qk = lax.dot_general(q_ref[:, ls], kk, (((1,), (1,)), ((), ())),
                             preferred_element_type=F32)
        nt = qk * (-LOG2_E * HEAD_DIM ** -0.5)
        l2_om = jnp.minimum(nt, 0.0) - jnp.log2(1.0 + jnp.exp2(-jnp.abs(nt)))
        if diagonal:
            l2_om = jnp.where(below2, l2_om, 0.0)
        l_b = l2_om.astype(BF16)
        tail = jnp.dot(l_b, tri_bd, preferred_element_type=F32)
        l2_w = (l2_om - nt) + tail
        sums = [tail[:, c:c + 1] + l_b[:, c:c + 1].astype(F32) for c in (0, TK)]
        return l2_w, sums

    def weighted_values(kb, p, l2_w, carry, diagonal):
        ks = pl.multiple_of(kb * TK, TK)
        if carry is not None:
            l2_w = l2_w + jnp.concatenate([jnp.broadcast_to(carry[0], (TQ, TK)),
                                           jnp.broadcast_to(carry[1], (TQ, TK))], axis=1)
        w = jnp.exp2(l2_w)
        if diagonal:
            w = jnp.where(below2, w, 0.0)
        vv = split_heads(v_ref[pl.ds(ks, TK), p * LANES:(p + 1) * LANES])
        return jnp.dot(w.astype(BF16), vv, preferred_element_type=F32)

    def sweep(n_blocks, kb0, resume):
        per_block = [[scores(kb0 - j, p, diagonal=(not resume and j == 0)) for p in pairs]
                     for j in range(n_blocks)]
        worst = None
        for p in pairs:
            ls = slice(p * LANES, (p + 1) * LANES)
            carry = [carry_ref[2 * p], carry_ref[2 * p + 1]] if resume else None
            acc = acc_ref[:, ls] if resume else None
            for j in range(n_blocks):
                l2_w, sums = per_block[j][p]
                pv = weighted_values(kb0 - j, p, l2_w, carry, diagonal=(not resume and j == 0))
                acc = pv if acc is None else acc + pv
                carry = sums if carry is None else [carry[0] + sums[0], carry[1] + sums[1]]
            acc_ref[:, ls] = acc
            carry_ref[2 * p] = carry[0]
            carry_ref[2 * p + 1] = carry[1]
            m = jnp.maximum(carry[0], carry[1])
            worst = m if worst is None else jnp.maximum(worst, m)
        return worst

    for n in range(1, HEAD_BLOCKS + 1):
        cond = (qi == n - 1) if n < HEAD_BLOCKS else (qi >= n - 1)

        @pl.when(cond)
        def _(n=n):
            worst_ref[...] = sweep(n, qi, resume=False)

    def keep_going(kb, worst):
        return jnp.logical_and(kb >= 0, worst > LOG2_WEIGHT_CUTOFF)

    def body(state):
        kb, _ = state
        return kb - 1, keep_going(kb - 1, jnp.max(sweep(1, kb, resume=True)))

    kb = qi - HEAD_BLOCKS
    lax.while_loop(lambda s: s[1], body, (kb, keep_going(kb, jnp.max(worst_ref[...]))))
    o_ref[...] = acc_ref[...].astype(o_ref.dtype)


def _attention(q, k, v):
    bsz, seq, d = q.shape
    return pl.pallas_call(
        _attn_kernel,
        grid=(bsz, d // ATTN_LANES, seq // TQ),
        in_specs=[pl.BlockSpec((None, TQ, ATTN_LANES), lambda b, p, i: (b, i, p)),
                  pl.BlockSpec((None, seq, ATTN_LANES), lambda b, p, i: (b, 0, p)),
                  pl.BlockSpec((None, seq, ATTN_LANES), lambda b, p, i: (b, 0, p))],
        out_specs=pl.BlockSpec((None, TQ, ATTN_LANES), lambda b, p, i: (b, i, p)),
        out_shape=jax.ShapeDtypeStruct((bsz, seq, d), INTER_DTYPE),
        scratch_shapes=[pltpu.VMEM((TQ, ATTN_LANES), F32),
                        pltpu.VMEM((ATTN_LANES // HEAD_DIM, TQ, 1), F32),
                        pltpu.VMEM((TQ, 1), F32)],
        compiler_params=pltpu.CompilerParams(
            dimension_semantics=("parallel", "parallel", "parallel"),
            vmem_limit_bytes=VMEM_LIMIT),
        name="stickbreak_attn",
    )(q, k, v)


def _expand_heads(a):
    left = lax.broadcasted_iota(jnp.int32, (a.shape[0], LANES), 1) < HEAD_DIM
    pieces = []
    for p in range(SSM_HEADS // HEADS_PER_BLOCK):
        h0 = p * HEADS_PER_BLOCK
        pieces.append(jnp.where(left, a[:, h0:h0 + 1], a[:, h0 + 1:h0 + 2]))
    return jnp.concatenate(pieces, axis=1)


def _ssd_kernel(xbc_ref, dt_ref, zs_ref, cw_ref, cb_ref, dtb_ref, alog_ref, dskip_ref, gain_ref,
                y_ref, state_ref, hist_ref):
    @pl.when(pl.program_id(1) == 0)
    def _():
        state_ref[...] = jnp.zeros_like(state_ref)
        hist_ref[...] = jnp.zeros_like(hist_ref)

    u = xbc_ref[...].astype(F32)
    ext = jnp.concatenate([hist_ref[...], u], axis=0)
    hist_ref[...] = u[CHUNK - SUBLANES:, :]
    conv = cb_ref[...] + cw_ref[CONV_WIDTH - 1:CONV_WIDTH, :] * u
    for kk in range(CONV_WIDTH - 1):
        off = SUBLANES - (CONV_WIDTH - 1) + kk
        shifted = pltpu.roll(ext, ext.shape[0] - off, axis=0)[:CHUNK, :]
        conv = conv + cw_ref[kk:kk + 1, :] * shifted
    xa = _silu(conv)
    xs = xa[:, :D_SSM]
    b_in = xa[:, D_SSM:D_SSM + SSM_GROUPS * SSM_STATE]
    c_in = xa[:, D_SSM + SSM_GROUPS * SSM_STATE:]

    dt = _softplus(dt_ref[...] + dtb_ref[...])
    log_decay = dt * (-jnp.exp(alog_ref[...]))
    row = lax.broadcasted_iota(jnp.int32, (CHUNK, CHUNK), 0)
    col = lax.broadcasted_iota(jnp.int32, (CHUNK, CHUNK), 1)
    causal = col <= row
    tri_inc = causal.astype(BF16)
    p1 = log_decay.astype(BF16)
    r1 = log_decay - p1.astype(F32)
    p2 = r1.astype(BF16)
    p3 = (r1 - p2.astype(F32)).astype(BF16)
    a_cum = (jnp.dot(tri_inc, p1, preferred_element_type=F32)
             + jnp.dot(tri_inc, p2, preferred_element_type=F32)
             + jnp.dot(tri_inc, p3, preferred_element_type=F32))
    a_cum_t = a_cum.T
    a_last = a_cum[CHUNK - 1:CHUNK, :]
    exp_a = _expand_heads(jnp.exp(a_cum))
    to_end = _expand_heads(jnp.exp(a_last - a_cum))
    xdt = xs * _expand_heads(dt)
    xdt_b = xdt.astype(BF16)
    xdt_end_b = (xdt * to_end).astype(BF16)

    left = lax.broadcasted_iota(jnp.int32, (CHUNK, LANES), 1) < HEAD_DIM
    gw = (SSM_HEADS // SSM_GROUPS) * HEAD_DIM
    y_diag, y_off = [], []
    for g in range(SSM_GROUPS):
        bg_t = b_in[:, g * SSM_STATE:(g + 1) * SSM_STATE].T.astype(BF16)
        cg = c_in[:, g * SSM_STATE:(g + 1) * SSM_STATE].astype(BF16)
        cb = jnp.dot(cg, bg_t, preferred_element_type=F32)
        cols = slice(g * gw, (g + 1) * gw)
        prev = state_ref[:, cols]
        y_off.append(jnp.dot(cg, prev.astype(BF16), preferred_element_type=F32) * exp_a[:, cols])
        new = jnp.dot(bg_t, xdt_end_b[:, cols], preferred_element_type=F32)
        state_ref[:, cols] = prev * exp_a[CHUNK - 1:CHUNK, cols] + new
        for p in range(SSM_HEADS // SSM_GROUPS // HEADS_PER_BLOCK):
            h0 = g * (SSM_HEADS // SSM_GROUPS) + p * HEADS_PER_BLOCK
            m = []
            for h in (h0, h0 + 1):
                seg = a_cum[:, h:h + 1] - a_cum_t[h:h + 1, :]
                m.append((cb * jnp.exp(jnp.where(causal, seg, -1e30))).astype(BF16))
            xp = xdt_b[:, h0 * HEAD_DIM:(h0 + HEADS_PER_BLOCK) * HEAD_DIM]
            zero = jnp.zeros_like(xp)
            rhs = jnp.concatenate([jnp.where(left, xp, zero), jnp.where(left, zero, xp)], axis=0)
            y_diag.append(jnp.dot(jnp.concatenate(m, axis=1), rhs, preferred_element_type=F32))

    y = jnp.concatenate(y_diag, axis=1) + jnp.concatenate(y_off, axis=1) + xs * dskip_ref[...]
    gated = y * _silu(zs_ref[...].astype(F32))
    y_ref[...] = (gated * _rms_scale(gated) * gain_ref[...]).astype(y_ref.dtype)


def _ssd(xbc, dt_raw, z_ssm, conv_w, conv_b, dt_bias, a_log, d_skip, gain):
    bsz, seq, _ = xbc.shape
    row = lambda b, c: (b, c, 0)
    const = lambda shape: pl.BlockSpec(shape, lambda b, c: (0, 0))
    pad = lambda a: jnp.pad(a, (0, DT_PAD - SSM_HEADS)).reshape(1, DT_PAD)
    return pl.pallas_call(
        _ssd_kernel,
        grid=(bsz, seq // CHUNK),
        in_specs=[pl.BlockSpec((None, CHUNK, D_XBC), row),
                  pl.BlockSpec((None, CHUNK, DT_PAD), row),
                  pl.BlockSpec((None, CHUNK, D_SSM), row),
                  const((CONV_WIDTH, D_XBC)), const((1, D_XBC)),
                  const((1, DT_PAD)), const((1, DT_PAD)),
                  const((1, D_SSM)), const((1, D_SSM))],
        out_specs=pl.BlockSpec((None, CHUNK, D_SSM), row),
        out_shape=jax.ShapeDtypeStruct((bsz, seq, D_SSM), BF16),
        scratch_shapes=[pltpu.VMEM((SSM_STATE, D_SSM), F32),
                        pltpu.VMEM((SUBLANES, D_XBC), F32)],
        compiler_params=pltpu.CompilerParams(dimension_semantics=("parallel", "arbitrary")),
        name="conv_ssd_norm",
    )(xbc, dt_raw, z_ssm, conv_w, conv_b.reshape(1, D_XBC), pad(dt_bias), pad(a_log),
      jnp.repeat(d_skip, HEAD_DIM).reshape(1, D_SSM), gain.reshape(1, D_SSM))


def _out_kernel(o_ref, za_ref, ys_ref, x_ref, gate_ref, sbg_ref, nfg_ref, w_ref, out_ref):
    o = o_ref[...].astype(F32)
    y_attn = (o * _rms_scale(o) * sbg_ref[...]) * _silu(za_ref[...].astype(F32))
    mixed = (jnp.dot(y_attn.astype(BF16), w_ref[:D_ATTN, :], preferred_element_type=F32)
             + jnp.dot(ys_ref[...], w_ref[D_ATTN:, :], preferred_element_type=F32))
    xo = x_ref[...] + gate_ref[...] * mixed
    out_ref[...] = xo * _rms_scale(xo) * nfg_ref[...]


def _out_projection(o, z_attn, y_ssm, x, mod4, sb_gain, nf_gain, w_out):
    bsz, seq, d = x.shape
    row = lambda b, i: (b, i, 0)
    const = lambda shape: pl.BlockSpec(shape, lambda b, i: (0, 0))
    return pl.pallas_call(
        _out_kernel,
        grid=(bsz, seq // TM_OUT),
        in_specs=[pl.BlockSpec((None, TM_OUT, D_ATTN), row),
                  pl.BlockSpec((None, TM_OUT, D_ATTN), row),
                  pl.BlockSpec((None, TM_OUT, D_SSM), row),
                  pl.BlockSpec((None, TM_OUT, d), row),
                  pl.BlockSpec((None, None, 1, d), lambda b, i: (b, 2, 0, 0)),
                  const((1, D_ATTN)), const((1, d)),
                  const((D_ATTN + D_SSM, d))],
        out_specs=pl.BlockSpec((None, TM_OUT, d), row),
        out_shape=jax.ShapeDtypeStruct((bsz, seq, d), F32),
        compiler_params=pltpu.CompilerParams(
            dimension_semantics=("parallel", "parallel"), vmem_limit_bytes=VMEM_LIMIT),
        name="out_proj_norm",
    )(o, z_attn, y_ssm, x, mod4, sb_gain, nf_gain, w_out)


def kernel(x, c, w_ada, b_ada, norm_in_gain, w_in, conv_w, conv_b, dt_bias, a_log, d_skip,
           sb_norm_gain, ssm_norm_gain, w_out, norm_f_gain):
    bsz, seq, d = x.shape
    depth = w_in.shape[0]
    n_dt = SSM_HEADS
    for layer in range(depth):
        mod = _modulation(c, w_ada[layer], b_ada[layer])
        mod4 = mod.reshape(bsz, 3, 1, d)
        w = w_in[layer]
        w_all = jnp.concatenate(
            [w[:, :4 * D_ATTN + D_XBC], w[:, 4 * D_ATTN + D_XBC + n_dt:],
             w[:, 4 * D_ATTN + D_XBC:4 * D_ATTN + D_XBC + n_dt],
             jnp.zeros((d, DT_PAD - n_dt), w.dtype)], axis=1).astype(BF16)
        q, k, v, z_attn, xbc, z_ssm, dt_raw = _in_projection(
            x, mod4, norm_in_gain[layer].reshape(1, d), w_all)
        o = _attention(q, k, v)
        y_ssm = _ssd(xbc, dt_raw, z_ssm, conv_w[layer], conv_b[layer], dt_bias[layer],
                     a_log[layer], d_skip[layer], ssm_norm_gain[layer])
        last = layer == depth - 1
        assert last, "final norm is fused into the last layer's out-projection"
        x = _out_projection(o, z_attn, y_ssm, x, mod4, sb_norm_gain[layer].reshape(1, D_ATTN),
                            norm_f_gain.reshape(1, d), w_out[layer].astype(BF16))
    return x
```

```python
import functools

import jax
import jax.numpy as jnp
from jax import lax
from jax.experimental import pallas as pl
from jax.experimental.pallas import tpu as pltpu

F32 = jnp.float32
BF16 = jnp.bfloat16

D_MODEL = 1024
D_ATTN = 1024
D_SSM = 1024
HEAD_DIM = 64
SSM_HEADS = 16
SSM_GROUPS = 2
SSM_STATE = 128
CONV_WIDTH = 4
D_XBC = D_SSM + 2 * SSM_GROUPS * SSM_STATE
NORM_EPS = 1e-6

LANES = 128
HIST = 8
HEADS_PER_BLOCK = LANES // HEAD_DIM
DT_PAD = LANES
W_MAIN = 4 * D_ATTN + D_XBC + D_SSM
W_ALL = W_MAIN + DT_PAD

TM_IN = 512
TM_OUT = 512
TQ = 128
TK = 128
ATTN_LANES = 512
HEAD_BLOCKS = 5
CHUNK = 128
LOG2_E = 1.4426950408889634
LOG2_WEIGHT_CUTOFF = -127.0
VMEM_LIMIT = 56 * 1024 * 1024

INTER_DTYPE = BF16


def _silu(x):
    return x * jax.nn.sigmoid(x)


def _softplus(x):
    return jnp.maximum(x, 0.0) + jnp.log1p(jnp.exp(-jnp.abs(x)))


def _rms_scale(x):
    return lax.rsqrt(jnp.mean(x * x, axis=-1, keepdims=True) + NORM_EPS)


def _mod_kernel(c_ref, w_ref, b_ref, o_ref):
    c_act = _silu(c_ref[...]).astype(BF16)
    o_ref[...] = jnp.dot(c_act, w_ref[...].astype(BF16), preferred_element_type=F32) + b_ref[...]


def _modulation(c, w_ada, b_ada):
    bsz, d = c.shape
    n = w_ada.shape[1]
    tn = 1024
    return pl.pallas_call(
        _mod_kernel,
        grid=(n // tn,),
        in_specs=[pl.BlockSpec((bsz, d), lambda j: (0, 0)),
                  pl.BlockSpec((d, tn), lambda j: (0, j)),
                  pl.BlockSpec((1, tn), lambda j: (0, j))],
        out_specs=pl.BlockSpec((bsz, tn), lambda j: (0, j)),
        out_shape=jax.ShapeDtypeStruct((bsz, n), F32),
        name="adaln_mod",
    )(c, w_ada, b_ada.reshape(1, n))


def _inproj_ssd_kernel(x_ref, shift_ref, scale_ref, gain_ref, w_ref, cw_ref, cb_ref, dtb_ref,
                       alog_ref, dskip_ref, sgain_ref,
                       q_ref, k_ref, v_ref, za_ref, y_ref, state_ref, hist_ref):
    @pl.when(pl.program_id(1) == 0)
    def _():
        state_ref[...] = jnp.zeros_like(state_ref)
        hist_ref[...] = jnp.zeros_like(hist_ref)

    x = x_ref[...]
    y = x * _rms_scale(x) * gain_ref[...]
    h = (y * (1.0 + scale_ref[...]) + shift_ref[...]).astype(BF16)

    def proj(lo, width):
        return jnp.dot(h, w_ref[:, lo:lo + width], preferred_element_type=F32)

    xa = _silu(_causal_conv(proj(4 * D_ATTN, D_XBC), hist_ref, cw_ref, cb_ref))
    z_ssm = proj(4 * D_ATTN + D_XBC, D_SSM)
    dt = _softplus(proj(W_MAIN, DT_PAD) + dtb_ref[...])
    attn_outs = (q_ref, k_ref, v_ref, za_ref)
    for c in range(TM_IN // CHUNK):
        out_ref = attn_outs[c]
        out_ref[...] = proj(c * D_ATTN, D_ATTN).astype(out_ref.dtype)
        rows = slice(c * CHUNK, (c + 1) * CHUNK)
        y_ref[rows, :] = _ssd_chunk(xa[rows], dt[rows], z_ssm[rows], alog_ref, dskip_ref,
                                    sgain_ref, state_ref)


def _in_projection_ssd(x, mod4, gain, w_all, conv_w, conv_b, dt_bias, a_log, d_skip, ssm_gain):
    bsz, seq, d = x.shape
    assert TM_IN // CHUNK == 4, "one attention-side projection is issued per SSD chunk"
    row = lambda b, i: (b, i, 0)
    const = lambda shape: pl.BlockSpec(shape, lambda b, i: (0, 0))
    mod_spec = lambda j: pl.BlockSpec((None, None, 1, d), lambda b, i: (b, j, 0, 0))
    pad = lambda a: jnp.pad(a, (0, DT_PAD - SSM_HEADS)).reshape(1, DT_PAD)

    def out(width, dtype):
        return (pl.BlockSpec((None, TM_IN, width), row),
                jax.ShapeDtypeStruct((bsz, seq, width), dtype))

    outs = [out(D_ATTN, BF16), out(D_ATTN, BF16), out(D_ATTN, BF16), out(D_ATTN, INTER_DTYPE),
            out(D_SSM, BF16)]
    return pl.pallas_call(
        _inproj_ssd_kernel,
        grid=(bsz, seq // TM_IN),
        in_specs=[pl.BlockSpec((None, TM_IN, d), row),
                  mod_spec(0), mod_spec(1), const((1, d)),
                  pl.BlockSpec((d, W_ALL), lambda b, i: (0, 0), pipeline_mode=pl.Buffered(1)),
                  const((CONV_WIDTH, D_XBC)), const((1, D_XBC)),
                  const((1, DT_PAD)), const((1, DT_PAD)),
                  const((1, D_SSM)), const((1, D_SSM))],
        out_specs=[o[0] for o in outs],
        out_shape=[o[1] for o in outs],
        scratch_shapes=[pltpu.VMEM((SSM_STATE, D_SSM), F32),
                        pltpu.VMEM((HIST, D_XBC), F32)],
        compiler_params=pltpu.CompilerParams(
            dimension_semantics=("parallel", "arbitrary"), vmem_limit_bytes=VMEM_LIMIT),
        name="norm_in_proj_ssd",
    )(x, mod4, mod4, gain, w_all, conv_w, conv_b.reshape(1, D_XBC), pad(dt_bias), pad(a_log),
      jnp.repeat(d_skip, HEAD_DIM).reshape(1, D_SSM), ssm_gain.reshape(1, D_SSM))


def _attn_kernel(q_ref, k_ref, v_ref, o_ref, acc_ref, carry_ref, worst_ref):
    qi = pl.program_id(2)
    row = lax.broadcasted_iota(jnp.int32, (TQ, TK), 0)
    col = lax.broadcasted_iota(jnp.int32, (TQ, TK), 1)
    below = col < row
    below2 = jnp.concatenate([below, below], axis=1)
    def block_diag(a):
        zero = jnp.zeros_like(a)
        return jnp.concatenate([jnp.concatenate([a, zero], axis=1),
                                jnp.concatenate([zero, a], axis=1)], axis=0)

    tri_bd = block_diag((row > col).astype(BF16))
    left = lax.broadcasted_iota(jnp.int32, (TK, LANES), 1) < HEAD_DIM

    def split_heads(a):
        z0 = jnp.zeros_like(a)
        return jnp.concatenate([jnp.where(left, a, z0), jnp.where(left, z0, a)], axis=0)

    pairs = range(ATTN_LANES // LANES)

    def scores(kb, p, diagonal):
        ks = pl.multiple_of(kb * TK, TK)
        ls = slice(p * LANES, (p + 1) * LANES)
        kk = split_heads(k_ref[pl.ds(ks, TK), ls])
        qk = lax.dot_general(q_ref[:, ls], kk, (((1,), (1,)), ((), ())),
                             preferred_element_type=F32)
        nt = qk * (-LOG2_E * HEAD_DIM ** -0.5)
        l2_om = jnp.minimum(nt, 0.0) - jnp.log2(1.0 + jnp.exp2(-jnp.abs(nt)))
        if diagonal:
            l2_om = jnp.where(below2, l2_om, 0.0)
        l_b = l2_om.astype(BF16)
        tail = jnp.dot(l_b, tri_bd, preferred_element_type=F32)
        l2_w = (l2_om - nt) + tail
        sums = [tail[:, c:c + 1] + l_b[:, c:c + 1].astype(F32) for c in (0, TK)]
        return l2_w, sums

    def weighted_values(kb, p, l2_w, carry, diagonal):
        ks = pl.multiple_of(kb * TK, TK)
        if carry is not None:
            l2_w = l2_w + jnp.concatenate([jnp.broadcast_to(carry[0], (TQ, TK)),
                                           jnp.broadcast_to(carry[1], (TQ, TK))], axis=1)
        w = jnp.exp2(l2_w)
        if diagonal:
            w = jnp.where(below2, w, 0.0)
        vv = split_heads(v_ref[pl.ds(ks, TK), p * LANES:(p + 1) * LANES])
        return jnp.dot(w.astype(BF16), vv, preferred_element_type=F32)

    def sweep(n_blocks, kb0, resume):
        per_block = [[scores(kb0 - j, p, diagonal=(not resume and j == 0)) for p in pairs]
                     for j in range(n_blocks)]
        worst = None
        for p in pairs:
            ls = slice(p * LANES, (p + 1) * LANES)
            carry = [carry_ref[2 * p], carry_ref[2 * p + 1]] if resume else None
            acc = acc_ref[:, ls] if resume else None
            for j in range(n_blocks):
                l2_w, sums = per_block[j][p]
                pv = weighted_values(kb0 - j, p, l2_w, carry, diagonal=(not resume and j == 0))
                acc = pv if acc is None else acc + pv
                carry = sums if carry is None else [carry[0] + sums[0], carry[1] + sums[1]]
            acc_ref[:, ls] = acc
            carry_ref[2 * p] = carry[0]
            carry_ref[2 * p + 1] = carry[1]
            m = jnp.maximum(carry[0], carry[1])
            worst = m if worst is None else jnp.maximum(worst, m)
        return worst

    for n in range(1, HEAD_BLOCKS + 1):
        cond = (qi == n - 1) if n < HEAD_BLOCKS else (qi >= n - 1)

        @pl.when(cond)
        def _(n=n):
            worst_ref[...] = sweep(n, qi, resume=False)

    def keep_going(kb, worst):
        return jnp.logical_and(kb >= 0, worst > LOG2_WEIGHT_CUTOFF)

    def body(state):
        kb, _ = state
        return kb - 1, keep_going(kb - 1, jnp.max(sweep(1, kb, resume=True)))

    kb = qi - HEAD_BLOCKS
    lax.while_loop(lambda s: s[1], body, (kb, keep_going(kb, jnp.max(worst_ref[...]))))
    o_ref[...] = acc_ref[...].astype(o_ref.dtype)


def _attention(q, k, v):
    bsz, seq, d = q.shape
    return pl.pallas_call(
        _attn_kernel,
        grid=(bsz, d // ATTN_LANES, seq // TQ),
        in_specs=[pl.BlockSpec((None, TQ, ATTN_LANES), lambda b, p, i: (b, i, p)),
                  pl.BlockSpec((None, seq, ATTN_LANES), lambda b, p, i: (b, 0, p)),
                  pl.BlockSpec((None, seq, ATTN_LANES), lambda b, p, i: (b, 0, p))],
        out_specs=pl.BlockSpec((None, TQ, ATTN_LANES), lambda b, p, i: (b, i, p)),
        out_shape=jax.ShapeDtypeStruct((bsz, seq, d), INTER_DTYPE),
        scratch_shapes=[pltpu.VMEM((TQ, ATTN_LANES), F32),
                        pltpu.VMEM((ATTN_LANES // HEAD_DIM, TQ, 1), F32),
                        pltpu.VMEM((TQ, 1), F32)],
        compiler_params=pltpu.CompilerParams(
            dimension_semantics=("parallel", "parallel", "parallel"),
            vmem_limit_bytes=VMEM_LIMIT),
        name="stickbreak_attn",
    )(q, k, v)


def _expand_heads(a):
    left = lax.broadcasted_iota(jnp.int32, (a.shape[0], LANES), 1) < HEAD_DIM
    pieces = []
    for p in range(SSM_HEADS // HEADS_PER_BLOCK):
        h0 = p * HEADS_PER_BLOCK
        pieces.append(jnp.where(left, a[:, h0:h0 + 1], a[:, h0 + 1:h0 + 2]))
    return jnp.concatenate(pieces, axis=1)


def _causal_conv(u, hist_ref, cw_ref, cb_ref):
    rows = u.shape[0]
    ext = jnp.concatenate([hist_ref[...], u], axis=0)
    hist_ref[...] = u[rows - HIST:, :]
    conv = cb_ref[...] + cw_ref[CONV_WIDTH - 1:CONV_WIDTH, :] * u
    for kk in range(CONV_WIDTH - 1):
        off = HIST - (CONV_WIDTH - 1) + kk
        shifted = pltpu.roll(ext, ext.shape[0] - off, axis=0)[:rows, :]
        conv = conv + cw_ref[kk:kk + 1, :] * shifted
    return conv


def _ssd_chunk(xa, dt, z_ssm, alog_ref, dskip_ref, gain_ref, state_ref):
    xs = xa[:, :D_SSM]
    b_in = xa[:, D_SSM:D_SSM + SSM_GROUPS * SSM_STATE]
    c_in = xa[:, D_SSM + SSM_GROUPS * SSM_STATE:]

    log_decay = dt * (-jnp.exp(alog_ref[...]))
    row = lax.broadcasted_iota(jnp.int32, (CHUNK, CHUNK), 0)
    col = lax.broadcasted_iota(jnp.int32, (CHUNK, CHUNK), 1)
    causal = col <= row
    tri_inc = causal.astype(BF16)
    p1 = log_decay.astype(BF16)
    r1 = log_decay - p1.astype(F32)
    p2 = r1.astype(BF16)
    p3 = (r1 - p2.astype(F32)).astype(BF16)
    a_cum = (jnp.dot(tri_inc, p1, preferred_element_type=F32)
             + jnp.dot(tri_inc, p2, preferred_element_type=F32)
             + jnp.dot(tri_inc, p3, preferred_element_type=F32))
    a_cum_t = a_cum.T
    a_last = a_cum[CHUNK - 1:CHUNK, :]
    exp_a = _expand_heads(jnp.exp(a_cum))
    to_end = _expand_heads(jnp.exp(a_last - a_cum))
    xdt = xs * _expand_heads(dt)
    xdt_b = xdt.astype(BF16)
    xdt_end_b = (xdt * to_end).astype(BF16)

    left = lax.broadcasted_iota(jnp.int32, (CHUNK, LANES), 1) < HEAD_DIM
    gw = (SSM_HEADS // SSM_GROUPS) * HEAD_DIM
    y_diag, y_off = [], []
    for g in range(SSM_GROUPS):
        bg_t = b_in[:, g * SSM_STATE:(g + 1) * SSM_STATE].T.astype(BF16)
        cg = c_in[:, g * SSM_STATE:(g + 1) * SSM_STATE].astype(BF16)
        cb = jnp.dot(cg, bg_t, preferred_element_type=F32)
        cols = slice(g * gw, (g + 1) * gw)
        prev = state_ref[:, cols]
        y_off.append(jnp.dot(cg, prev.astype(BF16), preferred_element_type=F32) * exp_a[:, cols])
        new = jnp.dot(bg_t, xdt_end_b[:, cols], preferred_element_type=F32)
        state_ref[:, cols] = prev * exp_a[CHUNK - 1:CHUNK, cols] + new
        for p in range(SSM_HEADS // SSM_GROUPS // HEADS_PER_BLOCK):
            h0 = g * (SSM_HEADS // SSM_GROUPS) + p * HEADS_PER_BLOCK
            m = []
            for h in (h0, h0 + 1):
                seg = a_cum[:, h:h + 1] - a_cum_t[h:h + 1, :]
                m.append((cb * jnp.exp(jnp.where(causal, seg, -1e30))).astype(BF16))
            xp = xdt_b[:, h0 * HEAD_DIM:(h0 + HEADS_PER_BLOCK) * HEAD_DIM]
            zero = jnp.zeros_like(xp)
            rhs = jnp.concatenate([jnp.where(left, xp, zero), jnp.where(left, zero, xp)], axis=0)
            y_diag.append(jnp.dot(jnp.concatenate(m, axis=1), rhs, preferred_element_type=F32))

    y = jnp.concatenate(y_diag, axis=1) + jnp.concatenate(y_off, axis=1) + xs * dskip_ref[...]
    gated = y * _silu(z_ssm)
    return (gated * _rms_scale(gated) * gain_ref[...]).astype(BF16)


def _out_kernel(o_ref, za_ref, ys_ref, x_ref, gate_ref, sbg_ref, nfg_ref, w_ref, out_ref):
    o = o_ref[...].astype(F32)
    y_attn = (o * _rms_scale(o) * sbg_ref[...]) * _silu(za_ref[...].astype(F32))
    mixed = (jnp.dot(y_attn.astype(BF16), w_ref[:D_ATTN, :], preferred_element_type=F32)
             + jnp.dot(ys_ref[...], w_ref[D_ATTN:, :], preferred_element_type=F32))
    xo = x_ref[...] + gate_ref[...] * mixed
    out_ref[...] = xo * _rms_scale(xo) * nfg_ref[...]


def _out_projection(o, z_attn, y_ssm, x, mod4, sb_gain, nf_gain, w_out):
    bsz, seq, d = x.shape
    row = lambda b, i: (b, i, 0)
    const = lambda shape: pl.BlockSpec(shape, lambda b, i: (0, 0))
    return pl.pallas_call(
        _out_kernel,
        grid=(bsz, seq // TM_OUT),
        in_specs=[pl.BlockSpec((None, TM_OUT, D_ATTN), row),
                  pl.BlockSpec((None, TM_OUT, D_ATTN), row),
                  pl.BlockSpec((None, TM_OUT, D_SSM), row),
                  pl.BlockSpec((None, TM_OUT, d), row),
                  pl.BlockSpec((None, None, 1, d), lambda b, i: (b, 2, 0, 0)),
                  const((1, D_ATTN)), const((1, d)),
                  const((D_ATTN + D_SSM, d))],
        out_specs=pl.BlockSpec((None, TM_OUT, d), row),
        out_shape=jax.ShapeDtypeStruct((bsz, seq, d), F32),
        compiler_params=pltpu.CompilerParams(
            dimension_semantics=("parallel", "parallel"), vmem_limit_bytes=VMEM_LIMIT),
        name="out_proj_norm",
    )(o, z_attn, y_ssm, x, mod4, sb_gain, nf_gain, w_out)


def kernel(x, c, w_ada, b_ada, norm_in_gain, w_in, conv_w, conv_b, dt_bias, a_log, d_skip,
           sb_norm_gain, ssm_norm_gain, w_out, norm_f_gain):
    bsz, seq, d = x.shape
    depth = w_in.shape[0]
    n_dt = SSM_HEADS
    for layer in range(depth):
        mod = _modulation(c, w_ada[layer], b_ada[layer])
        mod4 = mod.reshape(bsz, 3, 1, d)
        w = w_in[layer]
        w_all = jnp.concatenate(
            [w[:, :4 * D_ATTN + D_XBC], w[:, 4 * D_ATTN + D_XBC + n_dt:],
             w[:, 4 * D_ATTN + D_XBC:4 * D_ATTN + D_XBC + n_dt],
             jnp.zeros((d, DT_PAD - n_dt), w.dtype)], axis=1).astype(BF16)
        q, k, v, z_attn, y_ssm = _in_projection_ssd(
            x, mod4, norm_in_gain[layer].reshape(1, d), w_all, conv_w[layer], conv_b[layer],
            dt_bias[layer], a_log[layer], d_skip[layer], ssm_norm_gain[layer])
        o = _attention(q, k, v)
        last = layer == depth - 1
        assert last, "final norm is fused into the last layer's out-projection"
        x = _out_projection(o, z_attn, y_ssm, x, mod4, sb_norm_gain[layer].reshape(1, D_ATTN),
                            norm_f_gain.reshape(1, d), w_out[layer].astype(BF16))
    return x
```

```python
import functools

import jax
import jax.numpy as jnp
from jax import lax
from jax.experimental import pallas as pl
from jax.experimental.pallas import tpu as pltpu

F32 = jnp.float32
BF16 = jnp.bfloat16

D_MODEL = 1024
D_ATTN = 1024
D_SSM = 1024
HEAD_DIM = 64
SSM_HEADS = 16
SSM_GROUPS = 2
SSM_STATE = 128
CONV_WIDTH = 4
D_XBC = D_SSM + 2 * SSM_GROUPS * SSM_STATE
NORM_EPS = 1e-6

LANES = 128
HIST = 8
HEADS_PER_BLOCK = LANES // HEAD_DIM
DT_PAD = LANES
W_MAIN = 4 * D_ATTN + D_XBC + D_SSM
W_ALL = W_MAIN + DT_PAD

TM_IN = 512
TM_OUT = 512
TQ = 128
TK = 128
ATTN_LANES = 512
HEAD_BLOCKS = 5
CHUNK = 128
LOG2_E = 1.4426950408889634
LOG2_WEIGHT_CUTOFF = -127.0
VMEM_LIMIT = 56 * 1024 * 1024

INTER_DTYPE = BF16


def _silu(x):
    return x * jax.nn.sigmoid(x)


def _softplus(x):
    return jnp.maximum(x, 0.0) + jnp.log1p(jnp.exp(-jnp.abs(x)))


def _rms_scale(x):
    return lax.rsqrt(jnp.mean(x * x, axis=-1, keepdims=True) + NORM_EPS)


def _mod_kernel(c_ref, w_ref, b_ref, o_ref):
    c_act = _silu(c_ref[...]).astype(BF16)
    o_ref[...] = jnp.dot(c_act, w_ref[...].astype(BF16), preferred_element_type=F32) + b_ref[...]


def _modulation(c, w_ada, b_ada):
    bsz, d = c.shape
    n = w_ada.shape[1]
    tn = 1024
    return pl.pallas_call(
        _mod_kernel,
        grid=(n // tn,),
        in_specs=[pl.BlockSpec((bsz, d), lambda j: (0, 0)),
                  pl.BlockSpec((d, tn), lambda j: (0, j)),
                  pl.BlockSpec((1, tn), lambda j: (0, j))],
        out_specs=pl.BlockSpec((bsz, tn), lambda j: (0, j)),
        out_shape=jax.ShapeDtypeStruct((bsz, n), F32),
        name="adaln_mod",
    )(c, w_ada, b_ada.reshape(1, n))


def _inproj_ssd_kernel(x_ref, shift_ref, scale_ref, gain_ref, w_ref, cw_ref, cb_ref, dtb_ref,
                       alog_ref, dskip_ref, sgain_ref,
                       q_ref, k_ref, v_ref, za_ref, y_ref, state_ref, hist_ref):
    @pl.when(pl.program_id(1) == 0)
    def _():
        state_ref[...] = jnp.zeros_like(state_ref)
        hist_ref[...] = jnp.zeros_like(hist_ref)

    x = x_ref[...]
    y = x * _rms_scale(x) * gain_ref[...]
    h = (y * (1.0 + scale_ref[...]) + shift_ref[...]).astype(BF16)

    def proj(lo, width):
        return jnp.dot(h, w_ref[:, lo:lo + width], preferred_element_type=F32)

    xa = _silu(_causal_conv(proj(4 * D_ATTN, D_XBC), hist_ref, cw_ref, cb_ref))
    z_ssm = proj(4 * D_ATTN + D_XBC, D_SSM)
    dt = _softplus(proj(W_MAIN, DT_PAD) + dtb_ref[...])
    attn_outs = (q_ref, k_ref, v_ref, za_ref)
    for c in range(TM_IN // CHUNK):
        out_ref = attn_outs[c]
        out_ref[...] = proj(c * D_ATTN, D_ATTN).astype(out_ref.dtype)
        rows = slice(c * CHUNK, (c + 1) * CHUNK)
        y_ref[rows, :] = _ssd_chunk(xa[rows], dt[rows], z_ssm[rows], alog_ref, dskip_ref,
                                    sgain_ref, state_ref)


def _in_projection_ssd(x, mod4, gain, w_all, conv_w, conv_b, dt_bias, a_log, d_skip, ssm_gain):
    bsz, seq, d = x.shape
    assert TM_IN // CHUNK == 4, "one attention-side projection is issued per SSD chunk"
    row = lambda b, i: (b, i, 0)
    const = lambda shape: pl.BlockSpec(shape, lambda b, i: (0, 0))
    mod_spec = lambda j: pl.BlockSpec((None, None, 1, d), lambda b, i: (b, j, 0, 0))
    pad = lambda a: jnp.pad(a, (0, DT_PAD - SSM_HEADS)).reshape(1, DT_PAD)

    def out(width, dtype):
        return (pl.BlockSpec((None, TM_IN, width), row),
                jax.ShapeDtypeStruct((bsz, seq, width), dtype))

    outs = [out(D_ATTN, BF16), out(D_ATTN, BF16), out(D_ATTN, BF16), out(D_ATTN, INTER_DTYPE),
            out(D_SSM, BF16)]
    return pl.pallas_call(
        _inproj_ssd_kernel,
        grid=(bsz, seq // TM_IN),
        in_specs=[pl.BlockSpec((None, TM_IN, d), row),
                  mod_spec(0), mod_spec(1), const((1, d)),
                  pl.BlockSpec((d, W_ALL), lambda b, i: (0, 0), pipeline_mode=pl.Buffered(1)),
                  const((CONV_WIDTH, D_XBC)), const((1, D_XBC)),
                  const((1, DT_PAD)), const((1, DT_PAD)),
                  const((1, D_SSM)), const((1, D_SSM))],
        out_specs=[o[0] for o in outs],
        out_shape=[o[1] for o in outs],
        scratch_shapes=[pltpu.VMEM((SSM_STATE, D_SSM), F32),
                        pltpu.VMEM((HIST, D_XBC), F32)],
        compiler_params=pltpu.CompilerParams(
            dimension_semantics=("parallel", "arbitrary"), vmem_limit_bytes=VMEM_LIMIT),
        name="norm_in_proj_ssd",
    )(x, mod4, mod4, gain, w_all, conv_w, conv_b.reshape(1, D_XBC), pad(dt_bias), pad(a_log),
      jnp.repeat(d_skip, HEAD_DIM).reshape(1, D_SSM), ssm_gain.reshape(1, D_SSM))


def _attn_kernel(q_ref, k_ref, v_ref, o_ref, acc_ref, carry_ref, worst_ref, kk_ref, vv_ref):
    qi = pl.program_id(2)
    row = lax.broadcasted_iota(jnp.int32, (TQ, TK), 0)
    col = lax.broadcasted_iota(jnp.int32, (TQ, TK), 1)
    below = col < row
    below2 = jnp.concatenate([below, below], axis=1)
    def block_diag(a):
        zero = jnp.zeros_like(a)
        return jnp.concatenate([jnp.concatenate([a, zero], axis=1),
                                jnp.concatenate([zero, a], axis=1)], axis=0)

    tri_bd = block_diag((row > col).astype(BF16))
    left = lax.broadcasted_iota(jnp.int32, (TK, LANES), 1) < HEAD_DIM

    def split_heads(a):
        z0 = jnp.zeros_like(a)
        return jnp.concatenate([jnp.where(left, a, z0), jnp.where(left, z0, a)], axis=0)

    pairs = range(ATTN_LANES // LANES)

    @pl.when(qi == 0)
    def _():
        def fill(kb, _):
            ks = pl.multiple_of(kb * TK, TK)
            for p in pairs:
                ls = slice(p * LANES, (p + 1) * LANES)
                kk_ref[kb, p] = split_heads(k_ref[pl.ds(ks, TK), ls])
                vv_ref[kb, p] = split_heads(v_ref[pl.ds(ks, TK), ls])
            return 0

        lax.fori_loop(0, k_ref.shape[0] // TK, fill, 0)

    def scores(kb, p, diagonal):
        ls = slice(p * LANES, (p + 1) * LANES)
        qk = lax.dot_general(q_ref[:, ls], kk_ref[kb, p], (((1,), (1,)), ((), ())),
                             preferred_element_type=F32)
        nt = qk * (-LOG2_E * HEAD_DIM ** -0.5)
        l2_om = jnp.minimum(nt, 0.0) - jnp.log2(1.0 + jnp.exp2(-jnp.abs(nt)))
        if diagonal:
            l2_om = jnp.where(below2, l2_om, 0.0)
        l_b = l2_om.astype(BF16)
        tail = jnp.dot(l_b, tri_bd, preferred_element_type=F32)
        l2_w = (l2_om - nt) + tail
        sums = [tail[:, c:c + 1] + l2_om[:, c:c + 1] for c in (0, TK)]
        return l2_w, sums

    def weighted_values(kb, p, l2_w, carry, diagonal):
        if carry is not None:
            l2_w = l2_w + jnp.concatenate([jnp.broadcast_to(carry[0], (TQ, TK)),
                                           jnp.broadcast_to(carry[1], (TQ, TK))], axis=1)
        w = jnp.exp2(l2_w)
        if diagonal:
            w = jnp.where(below2, w, 0.0)
        return jnp.dot(w.astype(BF16), vv_ref[kb, p], preferred_element_type=F32)

    def sweep(n_blocks, kb0, resume):
        per_block = [[scores(kb0 - j, p, diagonal=(not resume and j == 0)) for p in pairs]
                     for j in range(n_blocks)]
        carry = [[carry_ref[2 * p], carry_ref[2 * p + 1]] if resume else None for p in pairs]
        acc = [acc_ref[:, p * LANES:(p + 1) * LANES] if resume else None for p in pairs]
        for j in range(n_blocks):
            for p in pairs:
                l2_w, sums = per_block[j][p]
                pv = weighted_values(kb0 - j, p, l2_w, carry[p],
                                     diagonal=(not resume and j == 0))
                acc[p] = pv if acc[p] is None else acc[p] + pv
                carry[p] = sums if carry[p] is None else [carry[p][0] + sums[0],
                                                          carry[p][1] + sums[1]]
        worst = None
        for p in pairs:
            acc_ref[:, p * LANES:(p + 1) * LANES] = acc[p]
            carry_ref[2 * p] = carry[p][0]
            carry_ref[2 * p + 1] = carry[p][1]
            m = jnp.maximum(carry[p][0], carry[p][1])
            worst = m if worst is None else jnp.maximum(worst, m)
        return worst

    for n in range(1, HEAD_BLOCKS + 1):
        cond = (qi == n - 1) if n < HEAD_BLOCKS else (qi >= n - 1)

        @pl.when(cond)
        def _(n=n):
            worst_ref[...] = sweep(n, qi, resume=False)

    def keep_going(kb, worst):
        return jnp.logical_and(kb >= 0, worst > LOG2_WEIGHT_CUTOFF)

    def body(state):
        kb, _ = state
        return kb - 1, keep_going(kb - 1, jnp.max(sweep(1, kb, resume=True)))

    kb = qi - HEAD_BLOCKS
    lax.while_loop(lambda s: s[1], body, (kb, keep_going(kb, jnp.max(worst_ref[...]))))
    o_ref[...] = acc_ref[...].astype(o_ref.dtype)


def _attention(q, k, v):
    bsz, seq, d = q.shape
    return pl.pallas_call(
        _attn_kernel,
        grid=(bsz, d // ATTN_LANES, seq // TQ),
        in_specs=[pl.BlockSpec((None, TQ, ATTN_LANES), lambda b, p, i: (b, i, p)),
                  pl.BlockSpec((None, seq, ATTN_LANES), lambda b, p, i: (b, 0, p)),
                  pl.BlockSpec((None, seq, ATTN_LANES), lambda b, p, i: (b, 0, p))],
        out_specs=pl.BlockSpec((None, TQ, ATTN_LANES), lambda b, p, i: (b, i, p)),
        out_shape=jax.ShapeDtypeStruct((bsz, seq, d), INTER_DTYPE),
        scratch_shapes=[pltpu.VMEM((TQ, ATTN_LANES), F32),
                        pltpu.VMEM((ATTN_LANES // HEAD_DIM, TQ, 1), F32),
                        pltpu.VMEM((TQ, 1), F32),
                        pltpu.VMEM((seq // TK, ATTN_LANES // LANES, 2 * TK, LANES), BF16),
                        pltpu.VMEM((seq // TK, ATTN_LANES // LANES, 2 * TK, LANES), BF16)],
        compiler_params=pltpu.CompilerParams(
            dimension_semantics=("parallel", "parallel", "arbitrary"),
            vmem_limit_bytes=VMEM_LIMIT),
        name="stickbreak_attn",
    )(q, k, v)


def _expand_heads(a):
    left = lax.broadcasted_iota(jnp.int32, (a.shape[0], LANES), 1) < HEAD_DIM
    pieces = []
    for p in range(SSM_HEADS // HEADS_PER_BLOCK):
        h0 = p * HEADS_PER_BLOCK
        pieces.append(jnp.where(left, a[:, h0:h0 + 1], a[:, h0 + 1:h0 + 2]))
    return jnp.concatenate(pieces, axis=1)


def _causal_conv(u, hist_ref, cw_ref, cb_ref):
    rows = u.shape[0]
    ext = jnp.concatenate([hist_ref[...], u], axis=0)
    hist_ref[...] = u[rows - HIST:, :]
    conv = cb_ref[...] + cw_ref[CONV_WIDTH - 1:CONV_WIDTH, :] * u
    for kk in range(CONV_WIDTH - 1):
        off = HIST - (CONV_WIDTH - 1) + kk
        shifted = pltpu.roll(ext, ext.shape[0] - off, axis=0)[:rows, :]
        conv = conv + cw_ref[kk:kk + 1, :] * shifted
    return conv


def _ssd_chunk(xa, dt, z_ssm, alog_ref, dskip_ref, gain_ref, state_ref):
    xs = xa[:, :D_SSM]
    b_in = xa[:, D_SSM:D_SSM + SSM_GROUPS * SSM_STATE]
    c_in = xa[:, D_SSM + SSM_GROUPS * SSM_STATE:]

    log_decay = dt * (-jnp.exp(alog_ref[...]))
    row = lax.broadcasted_iota(jnp.int32, (CHUNK, CHUNK), 0)
    col = lax.broadcasted_iota(jnp.int32, (CHUNK, CHUNK), 1)
    causal = col <= row
    tri_inc = causal.astype(BF16)
    p1 = log_decay.astype(BF16)
    r1 = log_decay - p1.astype(F32)
    p2 = r1.astype(BF16)
    p3 = (r1 - p2.astype(F32)).astype(BF16)
    a_cum = (jnp.dot(tri_inc, p1, preferred_element_type=F32)
             + jnp.dot(tri_inc, p2, preferred_element_type=F32)
             + jnp.dot(tri_inc, p3, preferred_element_type=F32))
    a_cum_t = a_cum.T
    a_last = a_cum[CHUNK - 1:CHUNK, :]
    exp_a = _expand_heads(jnp.exp(a_cum))
    to_end = _expand_heads(jnp.exp(a_last - a_cum))
    xdt = xs * _expand_heads(dt)
    xdt_b = xdt.astype(BF16)
    xdt_end_b = (xdt * to_end).astype(BF16)

    left = lax.broadcasted_iota(jnp.int32, (CHUNK, LANES), 1) < HEAD_DIM
    gw = (SSM_HEADS // SSM_GROUPS) * HEAD_DIM
    y_diag, y_off = [], []
    for g in range(SSM_GROUPS):
        bg_t = b_in[:, g * SSM_STATE:(g + 1) * SSM_STATE].T.astype(BF16)
        cg = c_in[:, g * SSM_STATE:(g + 1) * SSM_STATE].astype(BF16)
        cb = jnp.dot(cg, bg_t, preferred_element_type=F32)
        cols = slice(g * gw, (g + 1) * gw)
        prev = state_ref[:, cols]
        y_off.append(jnp.dot(cg, prev.astype(BF16), preferred_element_type=F32) * exp_a[:, cols])
        new = jnp.dot(bg_t, xdt_end_b[:, cols], preferred_element_type=F32)
        state_ref[:, cols] = prev * exp_a[CHUNK - 1:CHUNK, cols] + new
        for p in range(SSM_HEADS // SSM_GROUPS // HEADS_PER_BLOCK):
            h0 = g * (SSM_HEADS // SSM_GROUPS) + p * HEADS_PER_BLOCK
            m = []
            for h in (h0, h0 + 1):
                seg = a_cum[:, h:h + 1] - a_cum_t[h:h + 1, :]
                m.append((cb * jnp.exp(jnp.where(causal, seg, -1e30))).astype(BF16))
            xp = xdt_b[:, h0 * HEAD_DIM:(h0 + HEADS_PER_BLOCK) * HEAD_DIM]
            zero = jnp.zeros_like(xp)
            rhs = jnp.concatenate([jnp.where(left, xp, zero), jnp.where(left, zero, xp)], axis=0)
            y_diag.append(jnp.dot(jnp.concatenate(m, axis=1), rhs, preferred_element_type=F32))

    y = jnp.concatenate(y_diag, axis=1) + jnp.concatenate(y_off, axis=1) + xs * dskip_ref[...]
    gated = y * _silu(z_ssm)
    return (gated * _rms_scale(gated) * gain_ref[...]).astype(BF16)


def _out_kernel(o_ref, za_ref, ys_ref, x_ref, gate_ref, sbg_ref, nfg_ref, w_ref, out_ref):
    o = o_ref[...].astype(F32)
    y_attn = (o * _rms_scale(o) * sbg_ref[...]) * _silu(za_ref[...].astype(F32))
    mixed = (jnp.dot(y_attn.astype(BF16), w_ref[:D_ATTN, :], preferred_element_type=F32)
             + jnp.dot(ys_ref[...], w_ref[D_ATTN:, :], preferred_element_type=F32))
    xo = x_ref[...] + gate_ref[...] * mixed
    out_ref[...] = xo * _rms_scale(xo) * nfg_ref[...]


def _out_projection(o, z_attn, y_ssm, x, mod4, sb_gain, nf_gain, w_out):
    bsz, seq, d = x.shape
    row = lambda b, i: (b, i, 0)
    const = lambda shape: pl.BlockSpec(shape, lambda b, i: (0, 0))
    return pl.pallas_call(
        _out_kernel,
        grid=(bsz, seq // TM_OUT),
        in_specs=[pl.BlockSpec((None, TM_OUT, D_ATTN), row),
                  pl.BlockSpec((None, TM_OUT, D_ATTN), row),
                  pl.BlockSpec((None, TM_OUT, D_SSM), row),
                  pl.BlockSpec((None, TM_OUT, d), row),
                  pl.BlockSpec((None, None, 1, d), lambda b, i: (b, 2, 0, 0)),
                  const((1, D_ATTN)), const((1, d)),
                  const((D_ATTN + D_SSM, d))],
        out_specs=pl.BlockSpec((None, TM_OUT, d), row),
        out_shape=jax.ShapeDtypeStruct((bsz, seq, d), F32),
        compiler_params=pltpu.CompilerParams(
            dimension_semantics=("parallel", "parallel"), vmem_limit_bytes=VMEM_LIMIT),
        name="out_proj_norm",
    )(o, z_attn, y_ssm, x, mod4, sb_gain, nf_gain, w_out)


def kernel(x, c, w_ada, b_ada, norm_in_gain, w_in, conv_w, conv_b, dt_bias, a_log, d_skip,
           sb_norm_gain, ssm_norm_gain, w_out, norm_f_gain):
    bsz, seq, d = x.shape
    depth = w_in.shape[0]
    n_dt = SSM_HEADS
    for layer in range(depth):
        mod = _modulation(c, w_ada[layer], b_ada[layer])
        mod4 = mod.reshape(bsz, 3, 1, d)
        w = w_in[layer]
        w_all = jnp.concatenate(
            [w[:, :4 * D_ATTN + D_XBC], w[:, 4 * D_ATTN + D_XBC + n_dt:],
             w[:, 4 * D_ATTN + D_XBC:4 * D_ATTN + D_XBC + n_dt],
             jnp.zeros((d, DT_PAD - n_dt), w.dtype)], axis=1).astype(BF16)
        q, k, v, z_attn, y_ssm = _in_projection_ssd(
            x, mod4, norm_in_gain[layer].reshape(1, d), w_all, conv_w[layer], conv_b[layer],
            dt_bias[layer], a_log[layer], d_skip[layer], ssm_norm_gain[layer])
        o = _attention(q, k, v)
        last = layer == depth - 1
        assert last, "final norm is fused into the last layer's out-projection"
        x = _out_projection(o, z_attn, y_ssm, x, mod4, sb_norm_gain[layer].reshape(1, D_ATTN),
                            norm_f_gain.reshape(1, d), w_out[layer].astype(BF16))
    return x
```

```python
import functools

import jax
import jax.numpy as jnp
from jax import lax
from jax.experimental import pallas as pl
from jax.experimental.pallas import tpu as pltpu

F32 = jnp.float32
BF16 = jnp.bfloat16

D_MODEL = 1024
D_ATTN = 1024
D_SSM = 1024
HEAD_DIM = 64
SSM_HEADS = 16
SSM_GROUPS = 2
SSM_STATE = 128
CONV_WIDTH = 4
D_XBC = D_SSM + 2 * SSM_GROUPS * SSM_STATE
NORM_EPS = 1e-6

LANES = 128
HIST = 8
HEADS_PER_BLOCK = LANES // HEAD_DIM
DT_PAD = LANES
W_LEAD = 4 * D_ATTN + D_XBC

TM_IN = 512
TM_OUT = 512
TQ = 128
TK = 128
ATTN_LANES = 512
HEAD_BLOCKS = 5
CHUNK = 128
PIECE = 256
HOOKS_PER_CHUNK = 5
LOG2_E = 1.4426950408889634
LOG2_WEIGHT_CUTOFF = -127.0
VMEM_LIMIT = 56 * 1024 * 1024

INTER_DTYPE = BF16


def _silu(x):
    return x * jax.nn.sigmoid(x)


def _softplus(x):
    return jnp.maximum(x, 0.0) + jnp.log1p(jnp.exp(-jnp.abs(x)))


def _rms_scale(x):
    return lax.rsqrt(jnp.mean(x * x, axis=-1, keepdims=True) + NORM_EPS)


def _mod_kernel(c_ref, w_ref, b_ref, o_ref):
    c_act = _silu(c_ref[...]).astype(BF16)
    o_ref[...] = jnp.dot(c_act, w_ref[...].astype(BF16), preferred_element_type=F32) + b_ref[...]


def _modulation(c, w_ada, b_ada):
    bsz, d = c.shape
    n = w_ada.shape[1]
    tn = 1024
    return pl.pallas_call(
        _mod_kernel,
        grid=(n // tn,),
        in_specs=[pl.BlockSpec((bsz, d), lambda j: (0, 0)),
                  pl.BlockSpec((d, tn), lambda j: (0, j)),
                  pl.BlockSpec((1, tn), lambda j: (0, j))],
        out_specs=pl.BlockSpec((bsz, tn), lambda j: (0, j)),
        out_shape=jax.ShapeDtypeStruct((bsz, n), F32),
        name="adaln_mod",
    )(c, w_ada, b_ada.reshape(1, n))


def _inproj_ssd_kernel(x_ref, shift_ref, scale_ref, gain_ref, w_ref, wz_ref, wdt_ref,
                       cw_ref, cb_ref, dtb_ref, alog_ref, dskip_ref, sgain_ref,
                       q_ref, k_ref, v_ref, za_ref, y_ref, state_ref, hist_ref):
    @pl.when(pl.program_id(1) == 0)
    def _():
        state_ref[...] = jnp.zeros_like(state_ref)
        hist_ref[...] = jnp.zeros_like(hist_ref)

    x = x_ref[...]
    y = x * _rms_scale(x) * gain_ref[...]
    h = (y * (1.0 + scale_ref[...]) + shift_ref[...]).astype(BF16)

    def proj(lo, width, ref=w_ref):
        return jnp.dot(h, ref[:, lo:lo + width], preferred_element_type=F32)

    xa = _silu(_causal_conv(proj(4 * D_ATTN, D_XBC), hist_ref, cw_ref, cb_ref))
    dt = _softplus(proj(0, DT_PAD, wdt_ref) + dtb_ref[...])

    z_parts = []

    def z_piece(i):
        z_parts.append(proj(i * PIECE, PIECE, wz_ref))

    def attn_piece(out_ref, c, i):
        out_ref[:, i * PIECE:(i + 1) * PIECE] = proj(
            c * D_ATTN + i * PIECE, PIECE).astype(out_ref.dtype)

    tasks = [functools.partial(z_piece, i) for i in range(D_SSM // PIECE)]
    for c, out_ref in enumerate((q_ref, k_ref, v_ref, za_ref)):
        tasks += [functools.partial(attn_piece, out_ref, c, i) for i in range(D_ATTN // PIECE)]
    n_chunks = TM_IN // CHUNK
    assert len(tasks) == n_chunks * HOOKS_PER_CHUNK
    tasks = iter(tasks)
    for c in range(n_chunks):
        rows = slice(c * CHUNK, (c + 1) * CHUNK)
        y_ref[rows, :] = _ssd_chunk(
            xa[rows], dt[rows], lambda rows=rows: jnp.concatenate(z_parts, axis=1)[rows],
            alog_ref, dskip_ref, sgain_ref, state_ref, hook=lambda: next(tasks)())


def _in_projection_ssd(x, mod4, gain, w_in, conv_w, conv_b, dt_bias, a_log, d_skip, ssm_gain):
    bsz, seq, d = x.shape
    w_dt = jnp.pad(w_in[:, W_LEAD:W_LEAD + SSM_HEADS], ((0, 0), (0, DT_PAD - SSM_HEADS)))
    w_z = w_in[:, W_LEAD + SSM_HEADS:]
    row = lambda b, i: (b, i, 0)
    const = lambda shape: pl.BlockSpec(shape, lambda b, i: (0, 0))
    mod_spec = lambda j: pl.BlockSpec((None, None, 1, d), lambda b, i: (b, j, 0, 0))
    pad = lambda a: jnp.pad(a, (0, DT_PAD - SSM_HEADS)).reshape(1, DT_PAD)

    def out(width, dtype):
        return (pl.BlockSpec((None, TM_IN, width), row),
                jax.ShapeDtypeStruct((bsz, seq, width), dtype))

    outs = [out(D_ATTN, BF16), out(D_ATTN, BF16), out(D_ATTN, BF16), out(D_ATTN, INTER_DTYPE),
            out(D_SSM, BF16)]
    return pl.pallas_call(
        _inproj_ssd_kernel,
        grid=(bsz, seq // TM_IN),
        in_specs=[pl.BlockSpec((None, TM_IN, d), row),
                  mod_spec(0), mod_spec(1), const((1, d)),
                  pl.BlockSpec((d, W_LEAD), lambda b, i: (0, 0), pipeline_mode=pl.Buffered(1)),
                  pl.BlockSpec((d, D_SSM), lambda b, i: (0, 0), pipeline_mode=pl.Buffered(1)),
                  const((d, DT_PAD)),
                  const((CONV_WIDTH, D_XBC)), const((1, D_XBC)),
                  const((1, DT_PAD)), const((1, DT_PAD)),
                  const((1, D_SSM)), const((1, D_SSM))],
        out_specs=[o[0] for o in outs],
        out_shape=[o[1] for o in outs],
        scratch_shapes=[pltpu.VMEM((SSM_STATE, D_SSM), F32),
                        pltpu.VMEM((HIST, D_XBC), F32)],
        compiler_params=pltpu.CompilerParams(
            dimension_semantics=("parallel", "arbitrary"), vmem_limit_bytes=VMEM_LIMIT),
        name="norm_in_proj_ssd",
    )(x, mod4, mod4, gain, w_in, w_z, w_dt, conv_w, conv_b.reshape(1, D_XBC), pad(dt_bias),
      pad(a_log),
      jnp.repeat(d_skip, HEAD_DIM).reshape(1, D_SSM), ssm_gain.reshape(1, D_SSM))


def _attn_kernel(q_ref, k_ref, v_ref, o_ref, acc_ref, carry_ref, worst_ref, kk_ref, vv_ref):
    qi = pl.program_id(2)
    row = lax.broadcasted_iota(jnp.int32, (TQ, TK), 0)
    col = lax.broadcasted_iota(jnp.int32, (TQ, TK), 1)
    below = col < row
    below2 = jnp.concatenate([below, below], axis=1)
    def block_diag(a):
        zero = jnp.zeros_like(a)
        return jnp.concatenate([jnp.concatenate([a, zero], axis=1),
                                jnp.concatenate([zero, a], axis=1)], axis=0)

    tri_bd = block_diag((row > col).astype(BF16))
    left = lax.broadcasted_iota(jnp.int32, (TK, LANES), 1) < HEAD_DIM

    def split_heads(a):
        z0 = jnp.zeros_like(a)
        return jnp.concatenate([jnp.where(left, a, z0), jnp.where(left, z0, a)], axis=0)

    pairs = range(ATTN_LANES // LANES)

    @pl.when(qi == 0)
    def _():
        def fill(kb, _):
            ks = pl.multiple_of(kb * TK, TK)
            for p in pairs:
                ls = slice(p * LANES, (p + 1) * LANES)
                kk_ref[kb, p] = split_heads(k_ref[pl.ds(ks, TK), ls])
                vv_ref[kb, p] = split_heads(v_ref[pl.ds(ks, TK), ls])
            return 0

        lax.fori_loop(0, k_ref.shape[0] // TK, fill, 0)

    def scores(kb, p, diagonal):
        ls = slice(p * LANES, (p + 1) * LANES)
        qk = lax.dot_general(q_ref[:, ls], kk_ref[kb, p], (((1,), (1,)), ((), ())),
                             preferred_element_type=F32)
        nt = qk * (-LOG2_E * HEAD_DIM ** -0.5)
        l2_om = jnp.minimum(nt, 0.0) - jnp.log2(1.0 + jnp.exp2(-jnp.abs(nt)))
        if diagonal:
            l2_om = jnp.where(below2, l2_om, 0.0)
        l_b = l2_om.astype(BF16)
        tail = jnp.dot(l_b, tri_bd, preferred_element_type=F32)
        l2_w = (l2_om - nt) + tail
        sums = [tail[:, c:c + 1] + l2_om[:, c:c + 1] for c in (0, TK)]
        return l2_w, sums

    def weighted_values(kb, p, l2_w, carry, diagonal):
        if carry is not None:
            l2_w = l2_w + jnp.concatenate([jnp.broadcast_to(carry[0], (TQ, TK)),
                                           jnp.broadcast_to(carry[1], (TQ, TK))], axis=1)
        w = jnp.exp2(l2_w)
        if diagonal:
            w = jnp.where(below2, w, 0.0)
        return jnp.dot(w.astype(BF16), vv_ref[kb, p], preferred_element_type=F32)

    def sweep(n_blocks, kb0, resume):
        per_block = [[scores(kb0 - j, p, diagonal=(not resume and j == 0)) for p in pairs]
                     for j in range(n_blocks)]
        carry = [[carry_ref[2 * p], carry_ref[2 * p + 1]] if resume else None for p in pairs]
        acc = [acc_ref[:, p * LANES:(p + 1) * LANES] if resume else None for p in pairs]
        for j in range(n_blocks):
            for p in pairs:
                l2_w, sums = per_block[j][p]
                pv = weighted_values(kb0 - j, p, l2_w, carry[p],
                                     diagonal=(not resume and j == 0))
                acc[p] = pv if acc[p] is None else acc[p] + pv
                carry[p] = sums if carry[p] is None else [carry[p][0] + sums[0],
                                                          carry[p][1] + sums[1]]
        worst = None
        for p in pairs:
            acc_ref[:, p * LANES:(p + 1) * LANES] = acc[p]
            carry_ref[2 * p] = carry[p][0]
            carry_ref[2 * p + 1] = carry[p][1]
            m = jnp.maximum(carry[p][0], carry[p][1])
            worst = m if worst is None else jnp.maximum(worst, m)
        return worst

    for n in range(1, HEAD_BLOCKS + 1):
        cond = (qi == n - 1) if n < HEAD_BLOCKS else (qi >= n - 1)

        @pl.when(cond)
        def _(n=n):
            worst_ref[...] = sweep(n, qi, resume=False)

    def keep_going(kb, worst):
        return jnp.logical_and(kb >= 0, worst > LOG2_WEIGHT_CUTOFF)

    def body(state):
        kb, _ = state
        return kb - 1, keep_going(kb - 1, jnp.max(sweep(1, kb, resume=True)))

    kb = qi - HEAD_BLOCKS
    lax.while_loop(lambda s: s[1], body, (kb, keep_going(kb, jnp.max(worst_ref[...]))))
    o_ref[...] = acc_ref[...].astype(o_ref.dtype)


def _attention(q, k, v):
    bsz, seq, d = q.shape
    return pl.pallas_call(
        _attn_kernel,
        grid=(bsz, d // ATTN_LANES, seq // TQ),
        in_specs=[pl.BlockSpec((None, TQ, ATTN_LANES), lambda b, p, i: (b, i, p)),
                  pl.BlockSpec((None, seq, ATTN_LANES), lambda b, p, i: (b, 0, p)),
                  pl.BlockSpec((None, seq, ATTN_LANES), lambda b, p, i: (b, 0, p))],
        out_specs=pl.BlockSpec((None, TQ, ATTN_LANES), lambda b, p, i: (b, i, p)),
        out_shape=jax.ShapeDtypeStruct((bsz, seq, d), INTER_DTYPE),
        scratch_shapes=[pltpu.VMEM((TQ, ATTN_LANES), F32),
                        pltpu.VMEM((ATTN_LANES // HEAD_DIM, TQ, 1), F32),
                        pltpu.VMEM((TQ, 1), F32),
                        pltpu.VMEM((seq // TK, ATTN_LANES // LANES, 2 * TK, LANES), BF16),
                        pltpu.VMEM((seq // TK, ATTN_LANES // LANES, 2 * TK, LANES), BF16)],
        compiler_params=pltpu.CompilerParams(
            dimension_semantics=("parallel", "parallel", "arbitrary"),
            vmem_limit_bytes=VMEM_LIMIT),
        name="stickbreak_attn",
    )(q, k, v)


def _expand_heads(a):
    left = lax.broadcasted_iota(jnp.int32, (a.shape[0], LANES), 1) < HEAD_DIM
    pieces = []
    for p in range(SSM_HEADS // HEADS_PER_BLOCK):
        h0 = p * HEADS_PER_BLOCK
        pieces.append(jnp.where(left, a[:, h0:h0 + 1], a[:, h0 + 1:h0 + 2]))
    return jnp.concatenate(pieces, axis=1)


def _causal_conv(u, hist_ref, cw_ref, cb_ref):
    rows = u.shape[0]
    ext = jnp.concatenate([hist_ref[...], u], axis=0)
    hist_ref[...] = u[rows - HIST:, :]
    conv = cb_ref[...] + cw_ref[CONV_WIDTH - 1:CONV_WIDTH, :] * u
    for kk in range(CONV_WIDTH - 1):
        off = HIST - (CONV_WIDTH - 1) + kk
        shifted = pltpu.roll(ext, ext.shape[0] - off, axis=0)[:rows, :]
        conv = conv + cw_ref[kk:kk + 1, :] * shifted
    return conv


def _ssd_chunk(xa, dt, get_z, alog_ref, dskip_ref, gain_ref, state_ref, hook):
    xs = xa[:, :D_SSM]
    b_in = xa[:, D_SSM:D_SSM + SSM_GROUPS * SSM_STATE]
    c_in = xa[:, D_SSM + SSM_GROUPS * SSM_STATE:]

    log_decay = dt * (-jnp.exp(alog_ref[...]))
    row = lax.broadcasted_iota(jnp.int32, (CHUNK, CHUNK), 0)
    col = lax.broadcasted_iota(jnp.int32, (CHUNK, CHUNK), 1)
    causal = col <= row
    tri_inc = causal.astype(BF16)
    p1 = log_decay.astype(BF16)
    r1 = log_decay - p1.astype(F32)
    p2 = r1.astype(BF16)
    p3 = (r1 - p2.astype(F32)).astype(BF16)
    a_cum = (jnp.dot(tri_inc, p1, preferred_element_type=F32)
             + jnp.dot(tri_inc, p2, preferred_element_type=F32)
             + jnp.dot(tri_inc, p3, preferred_element_type=F32))
    a_cum_t = a_cum.T
    a_last = a_cum[CHUNK - 1:CHUNK, :]
    exp_a = _expand_heads(jnp.exp(a_cum))
    to_end = _expand_heads(jnp.exp(a_last - a_cum))
    xdt = xs * _expand_heads(dt)
    xdt_b = xdt.astype(BF16)
    xdt_end_b = (xdt * to_end).astype(BF16)

    left = lax.broadcasted_iota(jnp.int32, (CHUNK, LANES), 1) < HEAD_DIM
    gw = (SSM_HEADS // SSM_GROUPS) * HEAD_DIM
    y_diag, y_off = [], []
    hook()
    for g in range(SSM_GROUPS):
        bg_t = b_in[:, g * SSM_STATE:(g + 1) * SSM_STATE].T.astype(BF16)
        cg = c_in[:, g * SSM_STATE:(g + 1) * SSM_STATE].astype(BF16)
        cb = jnp.dot(cg, bg_t, preferred_element_type=F32)
        cols = slice(g * gw, (g + 1) * gw)
        prev = state_ref[:, cols]
        y_off.append(jnp.dot(cg, prev.astype(BF16), preferred_element_type=F32) * exp_a[:, cols])
        new = jnp.dot(bg_t, xdt_end_b[:, cols], preferred_element_type=F32)
        state_ref[:, cols] = prev * exp_a[CHUNK - 1:CHUNK, cols] + new
        for p in range(SSM_HEADS // SSM_GROUPS // HEADS_PER_BLOCK):
            h0 = g * (SSM_HEADS // SSM_GROUPS) + p * HEADS_PER_BLOCK
            m = []
            for h in (h0, h0 + 1):
                seg = a_cum[:, h:h + 1] - a_cum_t[h:h + 1, :]
                m.append((cb * jnp.exp(jnp.where(causal, seg, -1e30))).astype(BF16))
            xp = xdt_b[:, h0 * HEAD_DIM:(h0 + HEADS_PER_BLOCK) * HEAD_DIM]
            zero = jnp.zeros_like(xp)
            rhs = jnp.concatenate([jnp.where(left, xp, zero), jnp.where(left, zero, xp)], axis=0)
            y_diag.append(jnp.dot(jnp.concatenate(m, axis=1), rhs, preferred_element_type=F32))
            if p % 2 == 1:
                hook()

    y = jnp.concatenate(y_diag, axis=1) + jnp.concatenate(y_off, axis=1) + xs * dskip_ref[...]
    gated = y * _silu(get_z())
    return (gated * _rms_scale(gated) * gain_ref[...]).astype(BF16)


def _out_kernel(o_ref, za_ref, ys_ref, x_ref, gate_ref, sbg_ref, nfg_ref, w_ref, out_ref):
    o = o_ref[...].astype(F32)
    y_attn = (o * _rms_scale(o) * sbg_ref[...]) * _silu(za_ref[...].astype(F32))
    mixed = (jnp.dot(y_attn.astype(BF16), w_ref[:D_ATTN, :], preferred_element_type=F32)
             + jnp.dot(ys_ref[...], w_ref[D_ATTN:, :], preferred_element_type=F32))
    xo = x_ref[...] + gate_ref[...] * mixed
    out_ref[...] = xo * _rms_scale(xo) * nfg_ref[...]


def _out_projection(o, z_attn, y_ssm, x, mod4, sb_gain, nf_gain, w_out):
    bsz, seq, d = x.shape
    row = lambda b, i: (b, i, 0)
    const = lambda shape: pl.BlockSpec(shape, lambda b, i: (0, 0))
    return pl.pallas_call(
        _out_kernel,
        grid=(bsz, seq // TM_OUT),
        in_specs=[pl.BlockSpec((None, TM_OUT, D_ATTN), row),
                  pl.BlockSpec((None, TM_OUT, D_ATTN), row),
                  pl.BlockSpec((None, TM_OUT, D_SSM), row),
                  pl.BlockSpec((None, TM_OUT, d), row),
                  pl.BlockSpec((None, None, 1, d), lambda b, i: (b, 2, 0, 0)),
                  const((1, D_ATTN)), const((1, d)),
                  const((D_ATTN + D_SSM, d))],
        out_specs=pl.BlockSpec((None, TM_OUT, d), row),
        out_shape=jax.ShapeDtypeStruct((bsz, seq, d), F32),
        compiler_params=pltpu.CompilerParams(
            dimension_semantics=("parallel", "parallel"), vmem_limit_bytes=VMEM_LIMIT),
        name="out_proj_norm",
    )(o, z_attn, y_ssm, x, mod4, sb_gain, nf_gain, w_out)


def kernel(x, c, w_ada, b_ada, norm_in_gain, w_in, conv_w, conv_b, dt_bias, a_log, d_skip,
           sb_norm_gain, ssm_norm_gain, w_out, norm_f_gain):
    bsz, seq, d = x.shape
    depth = w_in.shape[0]
    for layer in range(depth):
        mod = _modulation(c, w_ada[layer], b_ada[layer])
        mod4 = mod.reshape(bsz, 3, 1, d)
        q, k, v, z_attn, y_ssm = _in_projection_ssd(
            x, mod4, norm_in_gain[layer].reshape(1, d), w_in[layer].astype(BF16),
            conv_w[layer], conv_b[layer],
            dt_bias[layer], a_log[layer], d_skip[layer], ssm_norm_gain[layer])
        o = _attention(q, k, v)
        last = layer == depth - 1
        assert last, "final norm is fused into the last layer's out-projection"
        x = _out_projection(o, z_attn, y_ssm, x, mod4, sb_norm_gain[layer].reshape(1, D_ATTN),
                            norm_f_gain.reshape(1, d), w_out[layer].astype(BF16))
    return x
```

```python
import functools

import jax
import jax.numpy as jnp
from jax import lax
from jax.experimental import pallas as pl
from jax.experimental.pallas import tpu as pltpu

F32 = jnp.float32
BF16 = jnp.bfloat16

D_MODEL = 1024
D_ATTN = 1024
D_SSM = 1024
HEAD_DIM = 64
SSM_HEADS = 16
SSM_GROUPS = 2
SSM_STATE = 128
CONV_WIDTH = 4
D_XBC = D_SSM + 2 * SSM_GROUPS * SSM_STATE
NORM_EPS = 1e-6

LANES = 128
HIST = 8
HEADS_PER_BLOCK = LANES // HEAD_DIM
DT_PAD = LANES
W_LEAD = 4 * D_ATTN + D_XBC

TM_IN = 512
TM_OUT = 512
TQ = 128
TK = 128
ATTN_LANES = 512
HEAD_BLOCKS = 5
CHUNK = 128
PIECE = 256
HOOKS_PER_CHUNK = 5
LOG2_E = 1.4426950408889634
LOG_WEIGHT_CUTOFF = -127.0 / LOG2_E
VMEM_LIMIT = 56 * 1024 * 1024

INTER_DTYPE = BF16


def _silu(x):
    return x * jax.nn.sigmoid(x)


def _softplus(x):
    return jnp.maximum(x, 0.0) + jnp.log1p(jnp.exp(-jnp.abs(x)))


def _rms_scale(x):
    return lax.rsqrt(jnp.mean(x * x, axis=-1, keepdims=True) + NORM_EPS)


def _mod_kernel(c_ref, w_ref, b_ref, o_ref):
    c_act = _silu(c_ref[...]).astype(BF16)
    o_ref[...] = jnp.dot(c_act, w_ref[...].astype(BF16), preferred_element_type=F32) + b_ref[...]


def _modulation(c, w_ada, b_ada):
    bsz, d = c.shape
    n = w_ada.shape[1]
    tn = 1024
    return pl.pallas_call(
        _mod_kernel,
        grid=(n // tn,),
        in_specs=[pl.BlockSpec((bsz, d), lambda j: (0, 0)),
                  pl.BlockSpec((d, tn), lambda j: (0, j)),
                  pl.BlockSpec((1, tn), lambda j: (0, j))],
        out_specs=pl.BlockSpec((bsz, tn), lambda j: (0, j)),
        out_shape=jax.ShapeDtypeStruct((bsz, n), F32),
        name="adaln_mod",
    )(c, w_ada, b_ada.reshape(1, n))


def _inproj_ssd_kernel(x_ref, shift_ref, scale_ref, gain_ref, w_ref, wz_ref, wdt_ref,
                       cw_ref, cb_ref, dtb_ref, alog_ref, dskip_ref, sgain_ref,
                       q_ref, k_ref, v_ref, za_ref, y_ref, state_ref, hist_ref):
    @pl.when(pl.program_id(1) == 0)
    def _():
        state_ref[...] = jnp.zeros_like(state_ref)
        hist_ref[...] = jnp.zeros_like(hist_ref)

    x = x_ref[...]
    y = x * _rms_scale(x) * gain_ref[...]
    h = (y * (1.0 + scale_ref[...]) + shift_ref[...]).astype(BF16)

    def proj(lo, width, ref=w_ref):
        return jnp.dot(h, ref[:, lo:lo + width], preferred_element_type=F32)

    xa = _silu(_causal_conv(proj(4 * D_ATTN, D_XBC), hist_ref, cw_ref, cb_ref))
    dt = _softplus(proj(0, DT_PAD, wdt_ref) + dtb_ref[...])

    z_parts = []

    def z_piece(i):
        z_parts.append(proj(i * PIECE, PIECE, wz_ref))

    def attn_piece(out_ref, c, i):
        res = proj(c * D_ATTN + i * PIECE, PIECE)
        if out_ref is q_ref:
            res = res * -(HEAD_DIM ** -0.5)
        out_ref[:, i * PIECE:(i + 1) * PIECE] = res.astype(out_ref.dtype)

    tasks = [functools.partial(z_piece, i) for i in range(D_SSM // PIECE)]
    for c, out_ref in enumerate((q_ref, k_ref, v_ref, za_ref)):
        tasks += [functools.partial(attn_piece, out_ref, c, i) for i in range(D_ATTN // PIECE)]
    n_chunks = TM_IN // CHUNK
    assert len(tasks) == n_chunks * HOOKS_PER_CHUNK
    tasks = iter(tasks)
    for c in range(n_chunks):
        rows = slice(c * CHUNK, (c + 1) * CHUNK)
        y_ref[rows, :] = _ssd_chunk(
            xa[rows], dt[rows], lambda rows=rows: jnp.concatenate(z_parts, axis=1)[rows],
            alog_ref, dskip_ref, sgain_ref, state_ref, hook=lambda: next(tasks)())


def _in_projection_ssd(x, mod4, gain, w_in, conv_w, conv_b, dt_bias, a_log, d_skip, ssm_gain):
    bsz, seq, d = x.shape
    w_dt = jnp.pad(w_in[:, W_LEAD:W_LEAD + SSM_HEADS], ((0, 0), (0, DT_PAD - SSM_HEADS)))
    w_z = w_in[:, W_LEAD + SSM_HEADS:]
    row = lambda b, i: (b, i, 0)
    const = lambda shape: pl.BlockSpec(shape, lambda b, i: (0, 0))
    mod_spec = lambda j: pl.BlockSpec((None, None, 1, d), lambda b, i: (b, j, 0, 0))
    pad = lambda a: jnp.pad(a, (0, DT_PAD - SSM_HEADS)).reshape(1, DT_PAD)

    def out(width, dtype):
        return (pl.BlockSpec((None, TM_IN, width), row),
                jax.ShapeDtypeStruct((bsz, seq, width), dtype))

    outs = [out(D_ATTN, BF16), out(D_ATTN, BF16), out(D_ATTN, BF16), out(D_ATTN, INTER_DTYPE),
            out(D_SSM, BF16)]
    return pl.pallas_call(
        _inproj_ssd_kernel,
        grid=(bsz, seq // TM_IN),
        in_specs=[pl.BlockSpec((None, TM_IN, d), row),
                  mod_spec(0), mod_spec(1), const((1, d)),
                  pl.BlockSpec((d, W_LEAD), lambda b, i: (0, 0), pipeline_mode=pl.Buffered(1)),
                  pl.BlockSpec((d, D_SSM), lambda b, i: (0, 0), pipeline_mode=pl.Buffered(1)),
                  const((d, DT_PAD)),
                  const((CONV_WIDTH, D_XBC)), const((1, D_XBC)),
                  const((1, DT_PAD)), const((1, DT_PAD)),
                  const((1, D_SSM)), const((1, D_SSM))],
        out_specs=[o[0] for o in outs],
        out_shape=[o[1] for o in outs],
        scratch_shapes=[pltpu.VMEM((SSM_STATE, D_SSM), F32),
                        pltpu.VMEM((HIST, D_XBC), F32)],
        compiler_params=pltpu.CompilerParams(
            dimension_semantics=("parallel", "arbitrary"), vmem_limit_bytes=VMEM_LIMIT),
        name="norm_in_proj_ssd",
    )(x, mod4, mod4, gain, w_in, w_z, w_dt, conv_w, conv_b.reshape(1, D_XBC), pad(dt_bias),
      pad(a_log),
      jnp.repeat(d_skip, HEAD_DIM).reshape(1, D_SSM), ssm_gain.reshape(1, D_SSM))


def _attn_kernel(q_ref, k_ref, v_ref, o_ref, acc_ref, carry_ref, worst_ref, kk_ref, vv_ref):
    qi = pl.program_id(2)
    row = lax.broadcasted_iota(jnp.int32, (TQ, TK), 0)
    col = lax.broadcasted_iota(jnp.int32, (TQ, TK), 1)
    below = col < row
    below2 = jnp.concatenate([below, below], axis=1)
    def block_diag(a):
        zero = jnp.zeros_like(a)
        return jnp.concatenate([jnp.concatenate([a, zero], axis=1),
                                jnp.concatenate([zero, a], axis=1)], axis=0)

    tri_bd = block_diag((row > col).astype(BF16))
    left = lax.broadcasted_iota(jnp.int32, (TK, LANES), 1) < HEAD_DIM

    def split_heads(a):
        z0 = jnp.zeros_like(a)
        return jnp.concatenate([jnp.where(left, a, z0), jnp.where(left, z0, a)], axis=0)

    pairs = range(ATTN_LANES // LANES)

    @pl.when(qi == 0)
    def _():
        def fill(kb, _):
            ks = pl.multiple_of(kb * TK, TK)
            for p in pairs:
                ls = slice(p * LANES, (p + 1) * LANES)
                kk_ref[kb, p] = split_heads(k_ref[pl.ds(ks, TK), ls])
                vv_ref[kb, p] = split_heads(v_ref[pl.ds(ks, TK), ls])
            return 0

        lax.fori_loop(0, k_ref.shape[0] // TK, fill, 0)

    def scores(kb, p, diagonal):
        ls = slice(p * LANES, (p + 1) * LANES)
        mz = lax.dot_general(q_ref[:, ls], kk_ref[kb, p], (((1,), (1,)), ((), ())),
                             preferred_element_type=F32)
        l_om = jnp.minimum(mz, 0.0) - jnp.log(1.0 + jnp.exp2(jnp.abs(mz) * -LOG2_E))
        if diagonal:
            l_om = jnp.where(below2, l_om, 0.0)
        l_b = l_om.astype(BF16)
        tail = jnp.dot(l_b, tri_bd, preferred_element_type=F32)
        l_w = (l_om - mz) + tail
        sums = [tail[:, c:c + 1] + l_om[:, c:c + 1] for c in (0, TK)]
        return l_w, sums

    def weighted_values(kb, p, l_w, carry, diagonal):
        if carry is not None:
            l_w = l_w + jnp.concatenate([jnp.broadcast_to(carry[0], (TQ, TK)),
                                         jnp.broadcast_to(carry[1], (TQ, TK))], axis=1)
        w = jnp.exp2(l_w * LOG2_E)
        if diagonal:
            w = jnp.where(below2, w, 0.0)
        return jnp.dot(w.astype(BF16), vv_ref[kb, p], preferred_element_type=F32)

    def sweep(n_blocks, kb0, resume):
        per_block = [[scores(kb0 - j, p, diagonal=(not resume and j == 0)) for p in pairs]
                     for j in range(n_blocks)]
        carry = [[carry_ref[2 * p], carry_ref[2 * p + 1]] if resume else None for p in pairs]
        acc = [acc_ref[:, p * LANES:(p + 1) * LANES] if resume else None for p in pairs]
        for j in range(n_blocks):
            for p in pairs:
                l_w, sums = per_block[j][p]
                pv = weighted_values(kb0 - j, p, l_w, carry[p],
                                     diagonal=(not resume and j == 0))
                acc[p] = pv if acc[p] is None else acc[p] + pv
                carry[p] = sums if carry[p] is None else [carry[p][0] + sums[0],
                                                          carry[p][1] + sums[1]]
        worst = None
        for p in pairs:
            acc_ref[:, p * LANES:(p + 1) * LANES] = acc[p]
            carry_ref[2 * p] = carry[p][0]
            carry_ref[2 * p + 1] = carry[p][1]
            m = jnp.maximum(carry[p][0], carry[p][1])
            worst = m if worst is None else jnp.maximum(worst, m)
        return worst

    for n in range(1, HEAD_BLOCKS + 1):
        cond = (qi == n - 1) if n < HEAD_BLOCKS else (qi >= n - 1)

        @pl.when(cond)
        def _(n=n):
            worst_ref[...] = sweep(n, qi, resume=False)

    def keep_going(kb, worst):
        return jnp.logical_and(kb >= 0, worst > LOG_WEIGHT_CUTOFF)

    def body(state):
        kb, _ = state
        return kb - 1, keep_going(kb - 1, jnp.max(sweep(1, kb, resume=True)))

    kb = qi - HEAD_BLOCKS
    lax.while_loop(lambda s: s[1], body, (kb, keep_going(kb, jnp.max(worst_ref[...]))))
    o_ref[...] = acc_ref[...].astype(o_ref.dtype)


def _attention(q, k, v):
    bsz, seq, d = q.shape
    return pl.pallas_call(
        _attn_kernel,
        grid=(bsz, d // ATTN_LANES, seq // TQ),
        in_specs=[pl.BlockSpec((None, TQ, ATTN_LANES), lambda b, p, i: (b, i, p)),
                  pl.BlockSpec((None, seq, ATTN_LANES), lambda b, p, i: (b, 0, p)),
                  pl.BlockSpec((None, seq, ATTN_LANES), lambda b, p, i: (b, 0, p))],
        out_specs=pl.BlockSpec((None, TQ, ATTN_LANES), lambda b, p, i: (b, i, p)),
        out_shape=jax.ShapeDtypeStruct((bsz, seq, d), INTER_DTYPE),
        scratch_shapes=[pltpu.VMEM((TQ, ATTN_LANES), F32),
                        pltpu.VMEM((ATTN_LANES // HEAD_DIM, TQ, 1), F32),
                        pltpu.VMEM((TQ, 1), F32),
                        pltpu.VMEM((seq // TK, ATTN_LANES // LANES, 2 * TK, LANES), BF16),
                        pltpu.VMEM((seq // TK, ATTN_LANES // LANES, 2 * TK, LANES), BF16)],
        compiler_params=pltpu.CompilerParams(
            dimension_semantics=("parallel", "parallel", "arbitrary"),
            vmem_limit_bytes=VMEM_LIMIT),
        name="stickbreak_attn",
    )(q, k, v)


def _expand_heads(a):
    left = lax.broadcasted_iota(jnp.int32, (a.shape[0], LANES), 1) < HEAD_DIM
    pieces = []
    for p in range(SSM_HEADS // HEADS_PER_BLOCK):
        h0 = p * HEADS_PER_BLOCK
        pieces.append(jnp.where(left, a[:, h0:h0 + 1], a[:, h0 + 1:h0 + 2]))
    return jnp.concatenate(pieces, axis=1)


def _causal_conv(u, hist_ref, cw_ref, cb_ref):
    rows = u.shape[0]
    ext = jnp.concatenate([hist_ref[...], u], axis=0)
    hist_ref[...] = u[rows - HIST:, :]
    conv = cb_ref[...] + cw_ref[CONV_WIDTH - 1:CONV_WIDTH, :] * u
    for kk in range(CONV_WIDTH - 1):
        off = HIST - (CONV_WIDTH - 1) + kk
        shifted = pltpu.roll(ext, ext.shape[0] - off, axis=0)[:rows, :]
        conv = conv + cw_ref[kk:kk + 1, :] * shifted
    return conv


def _ssd_chunk(xa, dt, get_z, alog_ref, dskip_ref, gain_ref, state_ref, hook):
    xs = xa[:, :D_SSM]
    b_in = xa[:, D_SSM:D_SSM + SSM_GROUPS * SSM_STATE]
    c_in = xa[:, D_SSM + SSM_GROUPS * SSM_STATE:]

    log_decay = dt * (-jnp.exp(alog_ref[...]))
    row = lax.broadcasted_iota(jnp.int32, (CHUNK, CHUNK), 0)
    col = lax.broadcasted_iota(jnp.int32, (CHUNK, CHUNK), 1)
    causal = col <= row
    tri_inc = causal.astype(BF16)
    p1 = log_decay.astype(BF16)
    r1 = log_decay - p1.astype(F32)
    p2 = r1.astype(BF16)
    p3 = (r1 - p2.astype(F32)).astype(BF16)
    a_cum = (jnp.dot(tri_inc, p1, preferred_element_type=F32)
             + jnp.dot(tri_inc, p2, preferred_element_type=F32)
             + jnp.dot(tri_inc, p3, preferred_element_type=F32))
    a_cum_t = a_cum.T
    a_full = _expand_heads(a_cum)
    exp_a = jnp.exp(a_full)
    to_end = jnp.exp(a_full[CHUNK - 1:CHUNK, :] - a_full)
    xdt = xs * _expand_heads(dt)
    xdt_b = xdt.astype(BF16)
    xdt_end_b = (xdt * to_end).astype(BF16)

    left = lax.broadcasted_iota(jnp.int32, (CHUNK, LANES), 1) < HEAD_DIM
    gw = (SSM_HEADS // SSM_GROUPS) * HEAD_DIM
    y_diag, y_off = [], []
    hook()
    for g in range(SSM_GROUPS):
        bg_t = b_in[:, g * SSM_STATE:(g + 1) * SSM_STATE].T.astype(BF16)
        cg = c_in[:, g * SSM_STATE:(g + 1) * SSM_STATE].astype(BF16)
        cb = jnp.dot(cg, bg_t, preferred_element_type=F32)
        cols = slice(g * gw, (g + 1) * gw)
        prev = state_ref[:, cols]
        y_off.append(jnp.dot(cg, prev.astype(BF16), preferred_element_type=F32) * exp_a[:, cols])
        new = jnp.dot(bg_t, xdt_end_b[:, cols], preferred_element_type=F32)
        state_ref[:, cols] = prev * exp_a[CHUNK - 1:CHUNK, cols] + new
        for p in range(SSM_HEADS // SSM_GROUPS // HEADS_PER_BLOCK):
            h0 = g * (SSM_HEADS // SSM_GROUPS) + p * HEADS_PER_BLOCK
            m = []
            for h in (h0, h0 + 1):
                seg = a_cum[:, h:h + 1] - a_cum_t[h:h + 1, :]
                m.append((cb * jnp.exp(jnp.where(causal, seg, -1e30))).astype(BF16))
            xp = xdt_b[:, h0 * HEAD_DIM:(h0 + HEADS_PER_BLOCK) * HEAD_DIM]
            zero = jnp.zeros_like(xp)
            rhs = jnp.concatenate([jnp.where(left, xp, zero), jnp.where(left, zero, xp)], axis=0)
            y_diag.append(jnp.dot(jnp.concatenate(m, axis=1), rhs, preferred_element_type=F32))
            if p % 2 == 1:
                hook()

    y = jnp.concatenate(y_diag, axis=1) + jnp.concatenate(y_off, axis=1) + xs * dskip_ref[...]
    gated = y * _silu(get_z())
    return (gated * _rms_scale(gated) * gain_ref[...]).astype(BF16)


def _out_kernel(o_ref, za_ref, ys_ref, x_ref, gate_ref, sbg_ref, nfg_ref, w_ref, out_ref):
    o = o_ref[...].astype(F32)
    y_attn = (o * _rms_scale(o) * sbg_ref[...]) * _silu(za_ref[...].astype(F32))
    mixed = (jnp.dot(y_attn.astype(BF16), w_ref[:D_ATTN, :], preferred_element_type=F32)
             + jnp.dot(ys_ref[...], w_ref[D_ATTN:, :], preferred_element_type=F32))
    xo = x_ref[...] + gate_ref[...] * mixed
    out_ref[...] = xo * _rms_scale(xo) * nfg_ref[...]


def _out_projection(o, z_attn, y_ssm, x, mod4, sb_gain, nf_gain, w_out):
    bsz, seq, d = x.shape
    row = lambda b, i: (b, i, 0)
    const = lambda shape: pl.BlockSpec(shape, lambda b, i: (0, 0))
    return pl.pallas_call(
        _out_kernel,
        grid=(bsz, seq // TM_OUT),
        in_specs=[pl.BlockSpec((None, TM_OUT, D_ATTN), row),
                  pl.BlockSpec((None, TM_OUT, D_ATTN), row),
                  pl.BlockSpec((None, TM_OUT, D_SSM), row),
                  pl.BlockSpec((None, TM_OUT, d), row),
                  pl.BlockSpec((None, None, 1, d), lambda b, i: (b, 2, 0, 0)),
                  const((1, D_ATTN)), const((1, d)),
                  const((D_ATTN + D_SSM, d))],
        out_specs=pl.BlockSpec((None, TM_OUT, d), row),
        out_shape=jax.ShapeDtypeStruct((bsz, seq, d), F32),
        compiler_params=pltpu.CompilerParams(
            dimension_semantics=("parallel", "parallel"), vmem_limit_bytes=VMEM_LIMIT),
        name="out_proj_norm",
    )(o, z_attn, y_ssm, x, mod4, sb_gain, nf_gain, w_out)


def kernel(x, c, w_ada, b_ada, norm_in_gain, w_in, conv_w, conv_b, dt_bias, a_log, d_skip,
           sb_norm_gain, ssm_norm_gain, w_out, norm_f_gain):
    bsz, seq, d = x.shape
    depth = w_in.shape[0]
    for layer in range(depth):
        mod = _modulation(c, w_ada[layer], b_ada[layer])
        mod4 = mod.reshape(bsz, 3, 1, d)
        q, k, v, z_attn, y_ssm = _in_projection_ssd(
            x, mod4, norm_in_gain[layer].reshape(1, d), w_in[layer].astype(BF16),
            conv_w[layer], conv_b[layer],
            dt_bias[layer], a_log[layer], d_skip[layer], ssm_norm_gain[layer])
        o = _attention(q, k, v)
        last = layer == depth - 1
        assert last, "final norm is fused into the last layer's out-projection"
        x = _out_projection(o, z_attn, y_ssm, x, mod4, sb_norm_gain[layer].reshape(1, D_ATTN),
                            norm_f_gain.reshape(1, d), w_out[layer].astype(BF16))
    return x
```

```python
import functools

import jax
import jax.numpy as jnp
from jax import lax
from jax.experimental import pallas as pl
from jax.experimental.pallas import tpu as pltpu

F32 = jnp.float32
BF16 = jnp.bfloat16

D_MODEL = 1024
D_ATTN = 1024
D_SSM = 1024
HEAD_DIM = 64
SSM_HEADS = 16
SSM_GROUPS = 2
SSM_STATE = 128
CONV_WIDTH = 4
D_XBC = D_SSM + 2 * SSM_GROUPS * SSM_STATE
NORM_EPS = 1e-6

LANES = 128
HIST = 8
HEADS_PER_BLOCK = LANES // HEAD_DIM
DT_PAD = LANES
W_LEAD = 4 * D_ATTN + D_XBC

TM_IN = 512
TM_OUT = 512
TQ = 128
TK = 128
ATTN_LANES = 512
SCORE_LEAD = 1
HEAD_BLOCKS = 5
CHUNK = 128
PIECE = 256
HOOKS_PER_CHUNK = 5
LOG2_E = 1.4426950408889634
LOG_WEIGHT_CUTOFF = -127.0 / LOG2_E
VMEM_LIMIT = 56 * 1024 * 1024

INTER_DTYPE = BF16


def _silu(x):
    return x * jax.nn.sigmoid(x)


def _softplus(x):
    return jnp.maximum(x, 0.0) + jnp.log1p(jnp.exp(-jnp.abs(x)))


def _rms_scale(x):
    return lax.rsqrt(jnp.mean(x * x, axis=-1, keepdims=True) + NORM_EPS)


def _mod_kernel(c_ref, w_ref, b_ref, o_ref):
    c_act = _silu(c_ref[...]).astype(BF16)
    o_ref[...] = jnp.dot(c_act, w_ref[...].astype(BF16), preferred_element_type=F32) + b_ref[...]


def _modulation(c, w_ada, b_ada):
    bsz, d = c.shape
    n = w_ada.shape[1]
    tn = 1024
    return pl.pallas_call(
        _mod_kernel,
        grid=(n // tn,),
        in_specs=[pl.BlockSpec((bsz, d), lambda j: (0, 0)),
                  pl.BlockSpec((d, tn), lambda j: (0, j)),
                  pl.BlockSpec((1, tn), lambda j: (0, j))],
        out_specs=pl.BlockSpec((bsz, tn), lambda j: (0, j)),
        out_shape=jax.ShapeDtypeStruct((bsz, n), F32),
        name="adaln_mod",
    )(c, w_ada, b_ada.reshape(1, n))


def _inproj_ssd_kernel(x_ref, shift_ref, scale_ref, gain_ref, w_ref, wz_ref, wdt_ref,
                       cw_ref, cb_ref, dtb_ref, alog_ref, dskip_ref, sgain_ref,
                       q_ref, k_ref, v_ref, za_ref, y_ref, state_ref, hist_ref):
    @pl.when(pl.program_id(1) == 0)
    def _():
        state_ref[...] = jnp.zeros_like(state_ref)
        hist_ref[...] = jnp.zeros_like(hist_ref)

    x = x_ref[...]
    y = x * _rms_scale(x) * gain_ref[...]
    h = (y * (1.0 + scale_ref[...]) + shift_ref[...]).astype(BF16)

    def proj(lo, width, ref=w_ref):
        return jnp.dot(h, ref[:, lo:lo + width], preferred_element_type=F32)

    xa = _silu(_causal_conv(proj(4 * D_ATTN, D_XBC), hist_ref, cw_ref, cb_ref))
    dt = _softplus(proj(0, DT_PAD, wdt_ref) + dtb_ref[...])

    z_parts = []

    def z_piece(i):
        z_parts.append(proj(i * PIECE, PIECE, wz_ref))

    def attn_piece(out_ref, c, i):
        res = proj(c * D_ATTN + i * PIECE, PIECE)
        if out_ref is q_ref:
            res = res * -(HEAD_DIM ** -0.5)
        out_ref[:, i * PIECE:(i + 1) * PIECE] = res.astype(out_ref.dtype)

    tasks = [functools.partial(z_piece, i) for i in range(D_SSM // PIECE)]
    for c, out_ref in enumerate((q_ref, k_ref, v_ref, za_ref)):
        tasks += [functools.partial(attn_piece, out_ref, c, i) for i in range(D_ATTN // PIECE)]
    n_chunks = TM_IN // CHUNK
    assert len(tasks) == n_chunks * HOOKS_PER_CHUNK
    tasks = iter(tasks)
    for c in range(n_chunks):
        rows = slice(c * CHUNK, (c + 1) * CHUNK)
        y_ref[rows, :] = _ssd_chunk(
            xa[rows], dt[rows], lambda rows=rows: jnp.concatenate(z_parts, axis=1)[rows],
            alog_ref, dskip_ref, sgain_ref, state_ref, hook=lambda: next(tasks)())


def _in_projection_ssd(x, mod4, gain, w_in, conv_w, conv_b, dt_bias, a_log, d_skip, ssm_gain):
    bsz, seq, d = x.shape
    w_dt = jnp.pad(w_in[:, W_LEAD:W_LEAD + SSM_HEADS], ((0, 0), (0, DT_PAD - SSM_HEADS)))
    w_z = w_in[:, W_LEAD + SSM_HEADS:]
    row = lambda b, i: (b, i, 0)
    const = lambda shape: pl.BlockSpec(shape, lambda b, i: (0, 0))
    mod_spec = lambda j: pl.BlockSpec((None, None, 1, d), lambda b, i: (b, j, 0, 0))
    pad = lambda a: jnp.pad(a, (0, DT_PAD - SSM_HEADS)).reshape(1, DT_PAD)

    def out(width, dtype):
        return (pl.BlockSpec((None, TM_IN, width), row),
                jax.ShapeDtypeStruct((bsz, seq, width), dtype))

    outs = [out(D_ATTN, BF16), out(D_ATTN, BF16), out(D_ATTN, BF16), out(D_ATTN, INTER_DTYPE),
            out(D_SSM, BF16)]
    return pl.pallas_call(
        _inproj_ssd_kernel,
        grid=(bsz, seq // TM_IN),
        in_specs=[pl.BlockSpec((None, TM_IN, d), row),
                  mod_spec(0), mod_spec(1), const((1, d)),
                  pl.BlockSpec((d, W_LEAD), lambda b, i: (0, 0), pipeline_mode=pl.Buffered(1)),
                  pl.BlockSpec((d, D_SSM), lambda b, i: (0, 0), pipeline_mode=pl.Buffered(1)),
                  const((d, DT_PAD)),
                  const((CONV_WIDTH, D_XBC)), const((1, D_XBC)),
                  const((1, DT_PAD)), const((1, DT_PAD)),
                  const((1, D_SSM)), const((1, D_SSM))],
        out_specs=[o[0] for o in outs],
        out_shape=[o[1] for o in outs],
        scratch_shapes=[pltpu.VMEM((SSM_STATE, D_SSM), F32),
                        pltpu.VMEM((HIST, D_XBC), F32)],
        compiler_params=pltpu.CompilerParams(
            dimension_semantics=("parallel", "arbitrary"), vmem_limit_bytes=VMEM_LIMIT),
        name="norm_in_proj_ssd",
    )(x, mod4, mod4, gain, w_in, w_z, w_dt, conv_w, conv_b.reshape(1, D_XBC), pad(dt_bias),
      pad(a_log),
      jnp.repeat(d_skip, HEAD_DIM).reshape(1, D_SSM), ssm_gain.reshape(1, D_SSM))


def _attn_kernel(q_ref, k_ref, v_ref, o_ref, acc_ref, carry_ref, worst_ref, kk_ref, vv_ref):
    qi = pl.program_id(2)
    row = lax.broadcasted_iota(jnp.int32, (TQ, TK), 0)
    col = lax.broadcasted_iota(jnp.int32, (TQ, TK), 1)
    below = col < row
    below2 = jnp.concatenate([below, below], axis=1)
    def block_diag(a):
        zero = jnp.zeros_like(a)
        return jnp.concatenate([jnp.concatenate([a, zero], axis=1),
                                jnp.concatenate([zero, a], axis=1)], axis=0)

    tri_bd = block_diag((row > col).astype(BF16))
    left = lax.broadcasted_iota(jnp.int32, (TK, LANES), 1) < HEAD_DIM

    def split_heads(a):
        z0 = jnp.zeros_like(a)
        return jnp.concatenate([jnp.where(left, a, z0), jnp.where(left, z0, a)], axis=0)

    pairs = range(ATTN_LANES // LANES)

    @pl.when(qi == 0)
    def _():
        def fill(kb, _):
            ks = pl.multiple_of(kb * TK, TK)
            for p in pairs:
                ls = slice(p * LANES, (p + 1) * LANES)
                kk_ref[kb, p] = split_heads(k_ref[pl.ds(ks, TK), ls])
                vv_ref[kb, p] = split_heads(v_ref[pl.ds(ks, TK), ls])
            return 0

        lax.fori_loop(0, k_ref.shape[0] // TK, fill, 0)

    def scores(kb, p, diagonal):
        ls = slice(p * LANES, (p + 1) * LANES)
        mz = lax.dot_general(q_ref[:, ls], kk_ref[kb, p], (((1,), (1,)), ((), ())),
                             preferred_element_type=F32)
        l_om = jnp.minimum(mz, 0.0) - jnp.log(1.0 + jnp.exp2(jnp.abs(mz) * -LOG2_E))
        if diagonal:
            l_om = jnp.where(below2, l_om, 0.0)
        l_b = l_om.astype(BF16)
        tail = jnp.dot(l_b, tri_bd, preferred_element_type=F32)
        l_w = (l_om - mz) + tail
        sums = [tail[:, c:c + 1] + l_om[:, c:c + 1] for c in (0, TK)]
        return l_w, sums

    def weighted_values(kb, p, l_w, carry, diagonal):
        if carry is not None:
            l_w = l_w + jnp.concatenate([jnp.broadcast_to(carry[0], (TQ, TK)),
                                         jnp.broadcast_to(carry[1], (TQ, TK))], axis=1)
        w = jnp.exp2(l_w * LOG2_E)
        if diagonal:
            w = jnp.where(below2, w, 0.0)
        return jnp.dot(w.astype(BF16), vv_ref[kb, p], preferred_element_type=F32)

    def sweep(n_blocks, kb0, resume):
        carry = [[carry_ref[2 * p], carry_ref[2 * p + 1]] if resume else None for p in pairs]
        acc = [acc_ref[:, p * LANES:(p + 1) * LANES] if resume else None for p in pairs]
        block_scores = lambda j: [scores(kb0 - j, p, diagonal=(not resume and j == 0))
                                  for p in pairs]
        per_block = {j: block_scores(j) for j in range(min(SCORE_LEAD, n_blocks))}
        for j in range(n_blocks):
            if j + SCORE_LEAD < n_blocks:
                per_block[j + SCORE_LEAD] = block_scores(j + SCORE_LEAD)
            for p in pairs:
                l_w, sums = per_block[j][p]
                pv = weighted_values(kb0 - j, p, l_w, carry[p],
                                     diagonal=(not resume and j == 0))
                acc[p] = pv if acc[p] is None else acc[p] + pv
                carry[p] = sums if carry[p] is None else [carry[p][0] + sums[0],
                                                          carry[p][1] + sums[1]]
        worst = None
        for p in pairs:
            acc_ref[:, p * LANES:(p + 1) * LANES] = acc[p]
            carry_ref[2 * p] = carry[p][0]
            carry_ref[2 * p + 1] = carry[p][1]
            m = jnp.maximum(carry[p][0], carry[p][1])
            worst = m if worst is None else jnp.maximum(worst, m)
        return worst

    for n in range(1, HEAD_BLOCKS + 1):
        cond = (qi == n - 1) if n < HEAD_BLOCKS else (qi >= n - 1)

        @pl.when(cond)
        def _(n=n):
            worst_ref[...] = sweep(n, qi, resume=False)

    def keep_going(kb, worst):
        return jnp.logical_and(kb >= 0, worst > LOG_WEIGHT_CUTOFF)

    def body(state):
        kb, _ = state
        return kb - 1, keep_going(kb - 1, jnp.max(sweep(1, kb, resume=True)))

    kb = qi - HEAD_BLOCKS
    lax.while_loop(lambda s: s[1], body, (kb, keep_going(kb, jnp.max(worst_ref[...]))))
    o_ref[...] = acc_ref[...].astype(o_ref.dtype)


def _attention(q, k, v):
    bsz, seq, d = q.shape
    return pl.pallas_call(
        _attn_kernel,
        grid=(bsz, d // ATTN_LANES, seq // TQ),
        in_specs=[pl.BlockSpec((None, TQ, ATTN_LANES), lambda b, p, i: (b, i, p)),
                  pl.BlockSpec((None, seq, ATTN_LANES), lambda b, p, i: (b, 0, p)),
                  pl.BlockSpec((None, seq, ATTN_LANES), lambda b, p, i: (b, 0, p))],
        out_specs=pl.BlockSpec((None, TQ, ATTN_LANES), lambda b, p, i: (b, i, p)),
        out_shape=jax.ShapeDtypeStruct((bsz, seq, d), INTER_DTYPE),
        scratch_shapes=[pltpu.VMEM((TQ, ATTN_LANES), F32),
                        pltpu.VMEM((ATTN_LANES // HEAD_DIM, TQ, 1), F32),
                        pltpu.VMEM((TQ, 1), F32),
                        pltpu.VMEM((seq // TK, ATTN_LANES // LANES, 2 * TK, LANES), BF16),
                        pltpu.VMEM((seq // TK, ATTN_LANES // LANES, 2 * TK, LANES), BF16)],
        compiler_params=pltpu.CompilerParams(
            dimension_semantics=("parallel", "parallel", "arbitrary"),
            vmem_limit_bytes=VMEM_LIMIT),
        name="stickbreak_attn",
    )(q, k, v)


def _expand_heads(a):
    left = lax.broadcasted_iota(jnp.int32, (a.shape[0], LANES), 1) < HEAD_DIM
    pieces = []
    for p in range(SSM_HEADS // HEADS_PER_BLOCK):
        h0 = p * HEADS_PER_BLOCK
        pieces.append(jnp.where(left, a[:, h0:h0 + 1], a[:, h0 + 1:h0 + 2]))
    return jnp.concatenate(pieces, axis=1)


def _causal_conv(u, hist_ref, cw_ref, cb_ref):
    rows = u.shape[0]
    ext = jnp.concatenate([hist_ref[...], u], axis=0)
    hist_ref[...] = u[rows - HIST:, :]
    conv = cb_ref[...] + cw_ref[CONV_WIDTH - 1:CONV_WIDTH, :] * u
    for kk in range(CONV_WIDTH - 1):
        off = HIST - (CONV_WIDTH - 1) + kk
        shifted = pltpu.roll(ext, ext.shape[0] - off, axis=0)[:rows, :]
        conv = conv + cw_ref[kk:kk + 1, :] * shifted
    return conv


def _ssd_chunk(xa, dt, get_z, alog_ref, dskip_ref, gain_ref, state_ref, hook):
    xs = xa[:, :D_SSM]
    b_in = xa[:, D_SSM:D_SSM + SSM_GROUPS * SSM_STATE]
    c_in = xa[:, D_SSM + SSM_GROUPS * SSM_STATE:]

    log_decay = dt * (-jnp.exp(alog_ref[...]))
    row = lax.broadcasted_iota(jnp.int32, (CHUNK, CHUNK), 0)
    col = lax.broadcasted_iota(jnp.int32, (CHUNK, CHUNK), 1)
    causal = col <= row
    tri_inc = causal.astype(BF16)
    p1 = log_decay.astype(BF16)
    r1 = log_decay - p1.astype(F32)
    p2 = r1.astype(BF16)
    p3 = (r1 - p2.astype(F32)).astype(BF16)
    a_cum = (jnp.dot(tri_inc, p1, preferred_element_type=F32)
             + jnp.dot(tri_inc, p2, preferred_element_type=F32)
             + jnp.dot(tri_inc, p3, preferred_element_type=F32))
    a_cum_t = a_cum.T
    a_full = _expand_heads(a_cum)
    exp_a = jnp.exp(a_full)
    to_end = jnp.exp(a_full[CHUNK - 1:CHUNK, :] - a_full)
    xdt = xs * _expand_heads(dt)
    xdt_b = xdt.astype(BF16)
    xdt_end_b = (xdt * to_end).astype(BF16)

    left = lax.broadcasted_iota(jnp.int32, (CHUNK, LANES), 1) < HEAD_DIM
    gw = (SSM_HEADS // SSM_GROUPS) * HEAD_DIM
    y_diag, y_off = [], []
    hook()
    for g in range(SSM_GROUPS):
        bg_t = b_in[:, g * SSM_STATE:(g + 1) * SSM_STATE].T.astype(BF16)
        cg = c_in[:, g * SSM_STATE:(g + 1) * SSM_STATE].astype(BF16)
        cb = jnp.dot(cg, bg_t, preferred_element_type=F32)
        cols = slice(g * gw, (g + 1) * gw)
        prev = state_ref[:, cols]
        y_off.append(jnp.dot(cg, prev.astype(BF16), preferred_element_type=F32) * exp_a[:, cols])
        new = jnp.dot(bg_t, xdt_end_b[:, cols], preferred_element_type=F32)
        state_ref[:, cols] = prev * exp_a[CHUNK - 1:CHUNK, cols] + new
        for p in range(SSM_HEADS // SSM_GROUPS // HEADS_PER_BLOCK):
            h0 = g * (SSM_HEADS // SSM_GROUPS) + p * HEADS_PER_BLOCK
            m = []
            for h in (h0, h0 + 1):
                seg = a_cum[:, h:h + 1] - a_cum_t[h:h + 1, :]
                m.append((cb * jnp.exp(jnp.where(causal, seg, -1e30))).astype(BF16))
            xp = xdt_b[:, h0 * HEAD_DIM:(h0 + HEADS_PER_BLOCK) * HEAD_DIM]
            zero = jnp.zeros_like(xp)
            rhs = jnp.concatenate([jnp.where(left, xp, zero), jnp.where(left, zero, xp)], axis=0)
            y_diag.append(jnp.dot(jnp.concatenate(m, axis=1), rhs, preferred_element_type=F32))
            if p % 2 == 1:
                hook()

    y = jnp.concatenate(y_diag, axis=1) + jnp.concatenate(y_off, axis=1) + xs * dskip_ref[...]
    gated = y * _silu(get_z())
    return (gated * _rms_scale(gated) * gain_ref[...]).astype(BF16)


def _out_kernel(o_ref, za_ref, ys_ref, x_ref, gate_ref, sbg_ref, nfg_ref, w_ref, out_ref):
    n_slabs = 2
    half = TM_OUT // n_slabs
    mixed = []
    for r in range(n_slabs):
        rows = slice(r * half, (r + 1) * half)
        o = o_ref[rows, :].astype(F32)
        y_attn = ((o * _rms_scale(o) * sbg_ref[...])
                  * _silu(za_ref[rows, :].astype(F32))).astype(BF16)
        mixed.append(jnp.dot(y_attn, w_ref[:D_ATTN, :], preferred_element_type=F32)
                     + jnp.dot(ys_ref[rows, :], w_ref[D_ATTN:, :], preferred_element_type=F32))
    for r in range(n_slabs):
        rows = slice(r * half, (r + 1) * half)
        xo = x_ref[rows, :] + gate_ref[...] * mixed[r]
        out_ref[rows, :] = xo * _rms_scale(xo) * nfg_ref[...]


def _out_projection(o, z_attn, y_ssm, x, mod4, sb_gain, nf_gain, w_out):
    bsz, seq, d = x.shape
    row = lambda b, i: (b, i, 0)
    const = lambda shape: pl.BlockSpec(shape, lambda b, i: (0, 0))
    return pl.pallas_call(
        _out_kernel,
        grid=(bsz, seq // TM_OUT),
        in_specs=[pl.BlockSpec((None, TM_OUT, D_ATTN), row),
                  pl.BlockSpec((None, TM_OUT, D_ATTN), row),
                  pl.BlockSpec((None, TM_OUT, D_SSM), row),
                  pl.BlockSpec((None, TM_OUT, d), row),
                  pl.BlockSpec((None, None, 1, d), lambda b, i: (b, 2, 0, 0)),
                  const((1, D_ATTN)), const((1, d)),
                  const((D_ATTN + D_SSM, d))],
        out_specs=pl.BlockSpec((None, TM_OUT, d), row),
        out_shape=jax.ShapeDtypeStruct((bsz, seq, d), F32),
        compiler_params=pltpu.CompilerParams(
            dimension_semantics=("parallel", "parallel"), vmem_limit_bytes=VMEM_LIMIT),
        name="out_proj_norm",
    )(o, z_attn, y_ssm, x, mod4, sb_gain, nf_gain, w_out)


def kernel(x, c, w_ada, b_ada, norm_in_gain, w_in, conv_w, conv_b, dt_bias, a_log, d_skip,
           sb_norm_gain, ssm_norm_gain, w_out, norm_f_gain):
    bsz, seq, d = x.shape
    depth = w_in.shape[0]
    for layer in range(depth):
        mod = _modulation(c, w_ada[layer], b_ada[layer])
        mod4 = mod.reshape(bsz, 3, 1, d)
        q, k, v, z_attn, y_ssm = _in_projection_ssd(
            x, mod4, norm_in_gain[layer].reshape(1, d), w_in[layer].astype(BF16),
            conv_w[layer], conv_b[layer],
            dt_bias[layer], a_log[layer], d_skip[layer], ssm_norm_gain[layer])
        o = _attention(q, k, v)
        last = layer == depth - 1
        assert last, "final norm is fused into the last layer's out-projection"
        x = _out_projection(o, z_attn, y_ssm, x, mod4, sb_norm_gain[layer].reshape(1, D_ATTN),
                            norm_f_gain.reshape(1, d), w_out[layer].astype(BF16))
    return x
```

```python
import functools

import jax
import jax.numpy as jnp
from jax import lax
from jax.experimental import pallas as pl
from jax.experimental.pallas import tpu as pltpu

F32 = jnp.float32
BF16 = jnp.bfloat16

D_MODEL = 1024
D_ATTN = 1024
D_SSM = 1024
HEAD_DIM = 64
SSM_HEADS = 16
SSM_GROUPS = 2
SSM_STATE = 128
CONV_WIDTH = 4
D_XBC = D_SSM + 2 * SSM_GROUPS * SSM_STATE
NORM_EPS = 1e-6

LANES = 128
HIST = 8
HEADS_PER_BLOCK = LANES // HEAD_DIM
DT_PAD = LANES
W_LEAD = 4 * D_ATTN + D_XBC

TM_IN = 512
TM_OUT = 512
TQ = 128
TK = 128
ATTN_LANES = 512
SCORE_LEAD = 1
HEAD_BLOCKS = 5
CHUNK = 128
PIECE = 256
HOOKS_PER_CHUNK = 5
LOG2_E = 1.4426950408889634
LOG_WEIGHT_CUTOFF = -127.0 / LOG2_E
VMEM_LIMIT = 56 * 1024 * 1024

INTER_DTYPE = BF16


def _silu(x):
    return x * jax.nn.sigmoid(x)


def _softplus(x):
    return jnp.maximum(x, 0.0) + jnp.log1p(jnp.exp(-jnp.abs(x)))


def _rms_scale(x):
    return lax.rsqrt(jnp.mean(x * x, axis=-1, keepdims=True) + NORM_EPS)


def _mod_kernel(c_ref, w_ref, b_ref, o_ref):
    c_act = _silu(c_ref[...]).astype(BF16)
    o_ref[...] = jnp.dot(c_act, w_ref[...].astype(BF16), preferred_element_type=F32) + b_ref[...]


def _modulation(c, w_ada, b_ada):
    bsz, d = c.shape
    n = w_ada.shape[1]
    tn = 1024
    return pl.pallas_call(
        _mod_kernel,
        grid=(n // tn,),
        in_specs=[pl.BlockSpec((bsz, d), lambda j: (0, 0)),
                  pl.BlockSpec((d, tn), lambda j: (0, j)),
                  pl.BlockSpec((1, tn), lambda j: (0, j))],
        out_specs=pl.BlockSpec((bsz, tn), lambda j: (0, j)),
        out_shape=jax.ShapeDtypeStruct((bsz, n), F32),
        name="adaln_mod",
    )(c, w_ada, b_ada.reshape(1, n))


def _inproj_ssd_kernel(x_ref, shift_ref, scale_ref, gain_ref, w_ref, wz_ref, wdt_ref,
                       cw_ref, cb_ref, dtb_ref, alog_ref, dskip_ref, sgain_ref,
                       q_ref, k_ref, v_ref, za_ref, y_ref, state_ref, hist_ref):
    @pl.when(pl.program_id(1) == 0)
    def _():
        state_ref[...] = jnp.zeros_like(state_ref)
        hist_ref[...] = jnp.zeros_like(hist_ref)

    x = x_ref[...]
    y = x * _rms_scale(x) * gain_ref[...]
    h = (y * (1.0 + scale_ref[...]) + shift_ref[...]).astype(BF16)

    def proj(lo, width, ref=w_ref):
        return jnp.dot(h, ref[:, lo:lo + width], preferred_element_type=F32)

    xa = _silu(_causal_conv(proj(4 * D_ATTN, D_XBC), hist_ref, cw_ref, cb_ref))
    dt = _softplus(proj(0, DT_PAD, wdt_ref) + dtb_ref[...])

    z_parts = []

    def z_piece(i):
        z_parts.append(proj(i * PIECE, PIECE, wz_ref))

    def attn_piece(out_ref, c, i):
        res = proj(c * D_ATTN + i * PIECE, PIECE)
        if out_ref is q_ref:
            res = res * -(HEAD_DIM ** -0.5)
        out_ref[:, i * PIECE:(i + 1) * PIECE] = res.astype(out_ref.dtype)

    tasks = [functools.partial(z_piece, i) for i in range(D_SSM // PIECE)]
    for c, out_ref in enumerate((q_ref, k_ref, v_ref, za_ref)):
        tasks += [functools.partial(attn_piece, out_ref, c, i) for i in range(D_ATTN // PIECE)]
    n_chunks = TM_IN // CHUNK
    assert len(tasks) == n_chunks * HOOKS_PER_CHUNK
    tasks = iter(tasks)
    for c in range(n_chunks):
        rows = slice(c * CHUNK, (c + 1) * CHUNK)
        y_ref[rows, :] = _ssd_chunk(
            xa[rows], dt[rows], lambda rows=rows: jnp.concatenate(z_parts, axis=1)[rows],
            alog_ref, dskip_ref, sgain_ref, state_ref, hook=lambda: next(tasks)())


def _in_projection_ssd(x, mod4, gain, w_in, conv_w, conv_b, dt_bias, a_log, d_skip, ssm_gain):
    bsz, seq, d = x.shape
    w_dt = jnp.pad(w_in[:, W_LEAD:W_LEAD + SSM_HEADS], ((0, 0), (0, DT_PAD - SSM_HEADS)))
    w_z = w_in[:, W_LEAD + SSM_HEADS:]
    row = lambda b, i: (b, i, 0)
    const = lambda shape: pl.BlockSpec(shape, lambda b, i: (0, 0))
    mod_spec = lambda j: pl.BlockSpec((None, None, 1, d), lambda b, i: (b, j, 0, 0))
    pad = lambda a: jnp.pad(a, (0, DT_PAD - SSM_HEADS)).reshape(1, DT_PAD)

    def out(width, dtype):
        return (pl.BlockSpec((None, TM_IN, width), row),
                jax.ShapeDtypeStruct((bsz, seq, width), dtype))

    outs = [out(D_ATTN, BF16), out(D_ATTN, BF16), out(D_ATTN, BF16), out(D_ATTN, INTER_DTYPE),
            out(D_SSM, BF16)]
    return pl.pallas_call(
        _inproj_ssd_kernel,
        grid=(bsz, seq // TM_IN),
        in_specs=[pl.BlockSpec((None, TM_IN, d), row),
                  mod_spec(0), mod_spec(1), const((1, d)),
                  pl.BlockSpec((d, W_LEAD), lambda b, i: (0, 0), pipeline_mode=pl.Buffered(1)),
                  pl.BlockSpec((d, D_SSM), lambda b, i: (0, 0), pipeline_mode=pl.Buffered(1)),
                  const((d, DT_PAD)),
                  const((CONV_WIDTH, D_XBC)), const((1, D_XBC)),
                  const((1, DT_PAD)), const((1, DT_PAD)),
                  const((1, D_SSM)), const((1, D_SSM))],
        out_specs=[o[0] for o in outs],
        out_shape=[o[1] for o in outs],
        scratch_shapes=[pltpu.VMEM((SSM_STATE, D_SSM), F32),
                        pltpu.VMEM((HIST, D_XBC), F32)],
        compiler_params=pltpu.CompilerParams(
            dimension_semantics=("parallel", "arbitrary"), vmem_limit_bytes=VMEM_LIMIT),
        name="norm_in_proj_ssd",
    )(x, mod4, mod4, gain, w_in, w_z, w_dt, conv_w, conv_b.reshape(1, D_XBC), pad(dt_bias),
      pad(a_log),
      jnp.repeat(d_skip, HEAD_DIM).reshape(1, D_SSM), ssm_gain.reshape(1, D_SSM))


def _attn_kernel(q_ref, k_ref, v_ref, o_ref, acc_ref, carry_ref, worst_ref, kk_ref, vv_ref):
    qi = pl.program_id(2)
    row = lax.broadcasted_iota(jnp.int32, (TQ, TK), 0)
    col = lax.broadcasted_iota(jnp.int32, (TQ, TK), 1)
    below = col < row
    below2 = jnp.concatenate([below, below], axis=1)
    def block_diag(a):
        zero = jnp.zeros_like(a)
        return jnp.concatenate([jnp.concatenate([a, zero], axis=1),
                                jnp.concatenate([zero, a], axis=1)], axis=0)

    tri_bd = block_diag((row > col).astype(BF16))
    left = lax.broadcasted_iota(jnp.int32, (TK, LANES), 1) < HEAD_DIM

    def split_heads(a):
        z0 = jnp.zeros_like(a)
        return jnp.concatenate([jnp.where(left, a, z0), jnp.where(left, z0, a)], axis=0)

    pairs = range(ATTN_LANES // LANES)

    def fill(kb):
        ks = pl.multiple_of(kb * TK, TK)
        for p in pairs:
            ls = slice(p * LANES, (p + 1) * LANES)
            kk_ref[kb, p] = split_heads(k_ref[pl.ds(ks, TK), ls])
            vv_ref[kb, p] = split_heads(v_ref[pl.ds(ks, TK), ls])

    def scores(kb, p, diagonal):
        ls = slice(p * LANES, (p + 1) * LANES)
        mz = lax.dot_general(q_ref[:, ls], kk_ref[kb, p], (((1,), (1,)), ((), ())),
                             preferred_element_type=F32)
        l_om = jnp.minimum(mz, 0.0) - jnp.log(1.0 + jnp.exp2(jnp.abs(mz) * -LOG2_E))
        if diagonal:
            l_om = jnp.where(below2, l_om, 0.0)
        l_b = l_om.astype(BF16)
        tail = jnp.dot(l_b, tri_bd, preferred_element_type=F32)
        l_w = (l_om - mz) + tail
        sums = [tail[:, c:c + 1] + l_om[:, c:c + 1] for c in (0, TK)]
        return l_w, sums

    def weighted_values(kb, p, l_w, carry, diagonal):
        if carry is not None:
            l_w = l_w + jnp.concatenate([jnp.broadcast_to(carry[0], (TQ, TK)),
                                         jnp.broadcast_to(carry[1], (TQ, TK))], axis=1)
        w = jnp.exp2(l_w * LOG2_E)
        if diagonal:
            w = jnp.where(below2, w, 0.0)
        return jnp.dot(w.astype(BF16), vv_ref[kb, p], preferred_element_type=F32)

    def sweep(n_blocks, kb0, resume):
        if not resume:
            fill(kb0)
        carry = [[carry_ref[2 * p], carry_ref[2 * p + 1]] if resume else None for p in pairs]
        acc = [acc_ref[:, p * LANES:(p + 1) * LANES] if resume else None for p in pairs]
        block_scores = lambda j: [scores(kb0 - j, p, diagonal=(not resume and j == 0))
                                  for p in pairs]
        per_block = {j: block_scores(j) for j in range(min(SCORE_LEAD, n_blocks))}
        for j in range(n_blocks):
            if j + SCORE_LEAD < n_blocks:
                per_block[j + SCORE_LEAD] = block_scores(j + SCORE_LEAD)
            for p in pairs:
                l_w, sums = per_block[j][p]
                pv = weighted_values(kb0 - j, p, l_w, carry[p],
                                     diagonal=(not resume and j == 0))
                acc[p] = pv if acc[p] is None else acc[p] + pv
                carry[p] = sums if carry[p] is None else [carry[p][0] + sums[0],
                                                          carry[p][1] + sums[1]]
        worst = None
        for p in pairs:
            acc_ref[:, p * LANES:(p + 1) * LANES] = acc[p]
            carry_ref[2 * p] = carry[p][0]
            carry_ref[2 * p + 1] = carry[p][1]
            m = jnp.maximum(carry[p][0], carry[p][1])
            worst = m if worst is None else jnp.maximum(worst, m)
        return worst

    for n in range(1, HEAD_BLOCKS + 1):
        cond = (qi == n - 1) if n < HEAD_BLOCKS else (qi >= n - 1)

        @pl.when(cond)
        def _(n=n):
            worst_ref[...] = sweep(n, qi, resume=False)

    def keep_going(kb, worst):
        return jnp.logical_and(kb >= 0, worst > LOG_WEIGHT_CUTOFF)

    def body(state):
        kb, _ = state
        return kb - 1, keep_going(kb - 1, jnp.max(sweep(1, kb, resume=True)))

    kb = qi - HEAD_BLOCKS
    lax.while_loop(lambda s: s[1], body, (kb, keep_going(kb, jnp.max(worst_ref[...]))))
    o_ref[...] = acc_ref[...].astype(o_ref.dtype)


def _attention(q, k, v):
    bsz, seq, d = q.shape
    return pl.pallas_call(
        _attn_kernel,
        grid=(bsz, d // ATTN_LANES, seq // TQ),
        in_specs=[pl.BlockSpec((None, TQ, ATTN_LANES), lambda b, p, i: (b, i, p)),
                  pl.BlockSpec((None, seq, ATTN_LANES), lambda b, p, i: (b, 0, p)),
                  pl.BlockSpec((None, seq, ATTN_LANES), lambda b, p, i: (b, 0, p))],
        out_specs=pl.BlockSpec((None, TQ, ATTN_LANES), lambda b, p, i: (b, i, p)),
        out_shape=jax.ShapeDtypeStruct((bsz, seq, d), INTER_DTYPE),
        scratch_shapes=[pltpu.VMEM((TQ, ATTN_LANES), F32),
                        pltpu.VMEM((ATTN_LANES // HEAD_DIM, TQ, 1), F32),
                        pltpu.VMEM((TQ, 1), F32),
                        pltpu.VMEM((seq // TK, ATTN_LANES // LANES, 2 * TK, LANES), BF16),
                        pltpu.VMEM((seq // TK, ATTN_LANES // LANES, 2 * TK, LANES), BF16)],
        compiler_params=pltpu.CompilerParams(
            dimension_semantics=("parallel", "parallel", "arbitrary"),
            vmem_limit_bytes=VMEM_LIMIT),
        name="stickbreak_attn",
    )(q, k, v)


def _expand_heads(a):
    left = lax.broadcasted_iota(jnp.int32, (a.shape[0], LANES), 1) < HEAD_DIM
    pieces = []
    for p in range(SSM_HEADS // HEADS_PER_BLOCK):
        h0 = p * HEADS_PER_BLOCK
        pieces.append(jnp.where(left, a[:, h0:h0 + 1], a[:, h0 + 1:h0 + 2]))
    return jnp.concatenate(pieces, axis=1)


def _causal_conv(u, hist_ref, cw_ref, cb_ref):
    rows = u.shape[0]
    ext = jnp.concatenate([hist_ref[...], u], axis=0)
    hist_ref[...] = u[rows - HIST:, :]
    conv = cb_ref[...] + cw_ref[CONV_WIDTH - 1:CONV_WIDTH, :] * u
    for kk in range(CONV_WIDTH - 1):
        off = HIST - (CONV_WIDTH - 1) + kk
        shifted = pltpu.roll(ext, ext.shape[0] - off, axis=0)[:rows, :]
        conv = conv + cw_ref[kk:kk + 1, :] * shifted
    return conv


def _ssd_chunk(xa, dt, get_z, alog_ref, dskip_ref, gain_ref, state_ref, hook):
    xs = xa[:, :D_SSM]
    b_in = xa[:, D_SSM:D_SSM + SSM_GROUPS * SSM_STATE]
    c_in = xa[:, D_SSM + SSM_GROUPS * SSM_STATE:]

    log_decay = dt * (-jnp.exp(alog_ref[...]))
    row = lax.broadcasted_iota(jnp.int32, (CHUNK, CHUNK), 0)
    col = lax.broadcasted_iota(jnp.int32, (CHUNK, CHUNK), 1)
    causal = col <= row
    tri_inc = causal.astype(BF16)
    p1 = log_decay.astype(BF16)
    r1 = log_decay - p1.astype(F32)
    p2 = r1.astype(BF16)
    p3 = (r1 - p2.astype(F32)).astype(BF16)
    a_cum = (jnp.dot(tri_inc, p1, preferred_element_type=F32)
             + jnp.dot(tri_inc, p2, preferred_element_type=F32)
             + jnp.dot(tri_inc, p3, preferred_element_type=F32))
    a_cum_t = a_cum.T
    a_full = _expand_heads(a_cum)
    exp_a = jnp.exp(a_full)
    to_end = jnp.exp(a_full[CHUNK - 1:CHUNK, :] - a_full)
    xdt = xs * _expand_heads(dt)
    xdt_b = xdt.astype(BF16)
    xdt_end_b = (xdt * to_end).astype(BF16)

    left = lax.broadcasted_iota(jnp.int32, (CHUNK, LANES), 1) < HEAD_DIM
    gw = (SSM_HEADS // SSM_GROUPS) * HEAD_DIM
    y_diag, y_off = [], []
    hook()
    for g in range(SSM_GROUPS):
        bg_t = b_in[:, g * SSM_STATE:(g + 1) * SSM_STATE].T.astype(BF16)
        cg = c_in[:, g * SSM_STATE:(g + 1) * SSM_STATE].astype(BF16)
        cb = jnp.dot(cg, bg_t, preferred_element_type=F32)
        cols = slice(g * gw, (g + 1) * gw)
        prev = state_ref[:, cols]
        y_off.append(jnp.dot(cg, prev.astype(BF16), preferred_element_type=F32) * exp_a[:, cols])
        new = jnp.dot(bg_t, xdt_end_b[:, cols], preferred_element_type=F32)
        state_ref[:, cols] = prev * exp_a[CHUNK - 1:CHUNK, cols] + new
        for p in range(SSM_HEADS // SSM_GROUPS // HEADS_PER_BLOCK):
            h0 = g * (SSM_HEADS // SSM_GROUPS) + p * HEADS_PER_BLOCK
            m = []
            for h in (h0, h0 + 1):
                seg = a_cum[:, h:h + 1] - a_cum_t[h:h + 1, :]
                m.append((cb * jnp.exp(jnp.where(causal, seg, -1e30))).astype(BF16))
            xp = xdt_b[:, h0 * HEAD_DIM:(h0 + HEADS_PER_BLOCK) * HEAD_DIM]
            zero = jnp.zeros_like(xp)
            rhs = jnp.concatenate([jnp.where(left, xp, zero), jnp.where(left, zero, xp)], axis=0)
            y_diag.append(jnp.dot(jnp.concatenate(m, axis=1), rhs, preferred_element_type=F32))
            if p % 2 == 1:
                hook()

    y = jnp.concatenate(y_diag, axis=1) + jnp.concatenate(y_off, axis=1) + xs * dskip_ref[...]
    gated = y * _silu(get_z())
    return (gated * _rms_scale(gated) * gain_ref[...]).astype(BF16)


def _out_kernel(o_ref, za_ref, ys_ref, x_ref, gate_ref, sbg_ref, nfg_ref, w_ref, out_ref):
    n_slabs = 2
    half = TM_OUT // n_slabs
    mixed = []
    for r in range(n_slabs):
        rows = slice(r * half, (r + 1) * half)
        o = o_ref[rows, :].astype(F32)
        y_attn = ((o * _rms_scale(o) * sbg_ref[...])
                  * _silu(za_ref[rows, :].astype(F32))).astype(BF16)
        mixed.append(jnp.dot(y_attn, w_ref[:D_ATTN, :], preferred_element_type=F32)
                     + jnp.dot(ys_ref[rows, :], w_ref[D_ATTN:, :], preferred_element_type=F32))
    for r in range(n_slabs):
        rows = slice(r * half, (r + 1) * half)
        xo = x_ref[rows, :] + gate_ref[...] * mixed[r]
        out_ref[rows, :] = xo * _rms_scale(xo) * nfg_ref[...]


def _out_projection(o, z_attn, y_ssm, x, mod4, sb_gain, nf_gain, w_out):
    bsz, seq, d = x.shape
    row = lambda b, i: (b, i, 0)
    const = lambda shape: pl.BlockSpec(shape, lambda b, i: (0, 0))
    return pl.pallas_call(
        _out_kernel,
        grid=(bsz, seq // TM_OUT),
        in_specs=[pl.BlockSpec((None, TM_OUT, D_ATTN), row),
                  pl.BlockSpec((None, TM_OUT, D_ATTN), row),
                  pl.BlockSpec((None, TM_OUT, D_SSM), row),
                  pl.BlockSpec((None, TM_OUT, d), row),
                  pl.BlockSpec((None, None, 1, d), lambda b, i: (b, 2, 0, 0)),
                  const((1, D_ATTN)), const((1, d)),
                  const((D_ATTN + D_SSM, d))],
        out_specs=pl.BlockSpec((None, TM_OUT, d), row),
        out_shape=jax.ShapeDtypeStruct((bsz, seq, d), F32),
        compiler_params=pltpu.CompilerParams(
            dimension_semantics=("parallel", "parallel"), vmem_limit_bytes=VMEM_LIMIT),
        name="out_proj_norm",
    )(o, z_attn, y_ssm, x, mod4, sb_gain, nf_gain, w_out)


def kernel(x, c, w_ada, b_ada, norm_in_gain, w_in, conv_w, conv_b, dt_bias, a_log, d_skip,
           sb_norm_gain, ssm_norm_gain, w_out, norm_f_gain):
    bsz, seq, d = x.shape
    depth = w_in.shape[0]
    for layer in range(depth):
        mod = _modulation(c, w_ada[layer], b_ada[layer])
        mod4 = mod.reshape(bsz, 3, 1, d)
        q, k, v, z_attn, y_ssm = _in_projection_ssd(
            x, mod4, norm_in_gain[layer].reshape(1, d), w_in[layer].astype(BF16),
            conv_w[layer], conv_b[layer],
            dt_bias[layer], a_log[layer], d_skip[layer], ssm_norm_gain[layer])
        o = _attention(q, k, v)
        last = layer == depth - 1
        assert last, "final norm is fused into the last layer's out-projection"
        x = _out_projection(o, z_attn, y_ssm, x, mod4, sb_norm_gain[layer].reshape(1, D_ATTN),
                            norm_f_gain.reshape(1, d), w_out[layer].astype(BF16))
    return x
```
